```python
import math
import jax, jax.numpy as jnp
from jax import lax
import numpy as np

D_MODEL = 2048
BATCH = 2
SEQ = 8192
DEPTH = 4

N_A_LAYERS = DEPTH // 2
N_B_LAYERS = DEPTH - N_A_LAYERS
NORM_EPS = 1e-6
GDN_HEADS = D_MODEL // 128
GDN_HEAD_DIM = 128
GDN_WIDTH = GDN_HEADS * GDN_HEAD_DIM
GDN_IN_WIDTH = 4 * GDN_WIDTH + 2 * GDN_HEADS
CONV_K = 4
CHUNK = 64
DIFF_HEADS = D_MODEL // 256
DIFF_HEAD_DIM = 128
DIFF_Q_WIDTH = DIFF_HEADS * 2 * DIFF_HEAD_DIM
DIFF_V_WIDTH = DIFF_HEADS * 2 * DIFF_HEAD_DIM
Q_BLOCK = 128
ROPE_THETA = 10000.0
PEER_HEADS = 8
PEER_QUERY_DIM = 256
PEER_HALF = PEER_QUERY_DIM // 2
N_KEYS = 128
N_EXPERTS = N_KEYS * N_KEYS
PEER_TOPK = 16
PEER_TOKEN_BLOCK = 128

kernel_name = 'yoco_gdn_diffattn_peer_adaln'


def rmsnorm(x, g):
    xf = x.astype(jnp.float32)
    y = xf * lax.rsqrt(jnp.mean(xf * xf, axis=-1, keepdims=True) + NORM_EPS)
    return (y * g.astype(jnp.float32)).astype(x.dtype)


def modulate(h, shift, scale):
    return h * (1 + scale[:, None, :]) + shift[:, None, :]


def adaln(c, w, b):
    return jax.nn.silu(c) @ w + b


def l2norm(x):
    xf = x.astype(jnp.float32)
    return xf * lax.rsqrt(jnp.sum(xf * xf, axis=-1, keepdims=True) + NORM_EPS)


def rope_tables(positions):
    inv_freq = ROPE_THETA ** (-jnp.arange(0, DIFF_HEAD_DIM, 2, dtype=jnp.float32) / DIFF_HEAD_DIM)
    ang = positions.astype(jnp.float32)[..., None] * inv_freq
    return jnp.cos(ang), jnp.sin(ang)


def apply_rope(x, cos, sin):
    cos = cos[:, :, None, None, :]
    sin = sin[:, :, None, None, :]
    xf = x.astype(jnp.float32)
    x1, x2 = jnp.split(xf, 2, axis=-1)
    return jnp.concatenate([x1 * cos - x2 * sin, x2 * cos + x1 * sin], axis=-1)


def causal_short_conv(x, w):
    s = x.shape[1]
    xp = jnp.pad(x, ((0, 0), (CONV_K - 1, 0), (0, 0)))
    out = xp[:, 0:s, :] * w[0]
    for i in range(1, CONV_K):
        out = out + xp[:, i:i + s, :] * w[i]
    return out


def chunk_gated_delta_rule(q, k, v, g, beta):
    b, s, nh, dk = q.shape
    dv = v.shape[-1]
    n = s // CHUNK

    def to_chunks(t):
        t = jnp.moveaxis(t.astype(jnp.float32), 2, 1)
        return t.reshape((b, nh, n, CHUNK) + t.shape[3:])

    q, k, v, g, beta = [to_chunks(t) for t in (q, k, v, g, beta)]
    G = jnp.cumsum(g, axis=-1)
    diff = G[..., :, None] - G[..., None, :]
    idx = jnp.arange(CHUNK)
    strict = idx[:, None] > idx[None, :]
    incl = idx[:, None] >= idx[None, :]
    kb = k * beta[..., None]
    L = jnp.einsum('bhnid,bhnjd->bhnij', kb, k) * jnp.exp(jnp.where(strict, diff, -jnp.inf))
    T = L + jnp.eye(CHUNK, dtype=jnp.float32)
    w = lax.linalg.triangular_solve(T, kb * jnp.exp(G)[..., None],
                                    left_side=True, lower=True, unit_diagonal=True)
    u_tilde = lax.linalg.triangular_solve(T, v * beta[..., None],
                                          left_side=True, lower=True, unit_diagonal=True)
    a_qk = jnp.einsum('bhnid,bhnjd->bhnij', q, k) * jnp.exp(jnp.where(incl, diff, -jnp.inf))
    q_g = q * jnp.exp(G)[..., None]
    k_d = k * jnp.exp(G[..., -1:] - G)[..., None]
    g_last = jnp.exp(G[..., -1])

    def step(state, inp):
        w_c, u_c, a_c, q_c, k_c, gl = inp
        u_new = u_c - jnp.einsum('bhck,bhkv->bhcv', w_c, state)
        o = jnp.einsum('bhck,bhkv->bhcv', q_c, state) + jnp.einsum('bhij,bhjv->bhiv', a_c, u_new)
        state = gl[..., None, None] * state + jnp.einsum('bhck,bhcv->bhkv', k_c, u_new)
        return state, o

    xs = tuple(jnp.moveaxis(t, 2, 0) for t in (w, u_tilde, a_qk, q_g, k_d, g_last))
    state0 = jnp.zeros((b, nh, dk, dv), jnp.float32)
    _, o = lax.scan(step, state0, xs)
    o = jnp.moveaxis(o, 0, 2).reshape(b, nh, s, dv)
    return jnp.moveaxis(o, 1, 2)


def gated_deltanet(h, w_in, conv_w, a_log, dt_bias, o_gain, w_out):
    b, s, _ = h.shape
    W = GDN_WIDTH
    proj = h @ w_in
    qkv = jax.nn.silu(causal_short_conv(proj[..., :3 * W], conv_w))
    z = proj[..., 3 * W:4 * W]
    a = proj[..., 4 * W:4 * W + GDN_HEADS]
    beta_logit = proj[..., 4 * W + GDN_HEADS:]
    q, k, v = [t.reshape(b, s, GDN_HEADS, GDN_HEAD_DIM) for t in jnp.split(qkv, 3, axis=-1)]
    q = l2norm(q) * GDN_HEAD_DIM ** -0.5
    k = l2norm(k)
    g = -jnp.exp(a_log.astype(jnp.float32)) * jax.nn.softplus(a.astype(jnp.float32) + dt_bias.astype(jnp.float32))
    beta = jax.nn.sigmoid(beta_logit.astype(jnp.float32))
    o = chunk_gated_delta_rule(q, k, v, g, beta)
    o = rmsnorm(o, o_gain) * jax.nn.silu(z.reshape(b, s, GDN_HEADS, GDN_HEAD_DIM).astype(jnp.float32))
    return o.reshape(b, s, W).astype(h.dtype) @ w_out


def shared_kv(x, c, norm_g, ada_w, ada_b, w_kv, cos, sin):
    b, s, _ = x.shape
    shift, scale = jnp.split(adaln(c, ada_w, ada_b), 2, axis=-1)
    h = modulate(rmsnorm(x, norm_g), shift, scale)
    kv = h @ w_kv
    k = apply_rope(kv[..., :DIFF_Q_WIDTH].reshape(b, s, DIFF_HEADS, 2, DIFF_HEAD_DIM), cos, sin)
    v = kv[..., DIFF_Q_WIDTH:].reshape(b, s, DIFF_HEADS, 2 * DIFF_HEAD_DIM).astype(jnp.float32)
    return jnp.transpose(k, (0, 2, 3, 1, 4)), jnp.transpose(v, (0, 2, 1, 3))


def diff_attention(h, k_sh, v_sh, cos, sin, w_q, lam_p, subln_g, w_out, lambda_init):
    b, s, _ = h.shape
    d = DIFF_HEAD_DIM
    q = apply_rope((h @ w_q).reshape(b, s, DIFF_HEADS, 2, d), cos, sin) * d ** -0.5
    lp = lam_p.astype(jnp.float32)
    lam = jnp.exp(jnp.sum(lp[0] * lp[1])) - jnp.exp(jnp.sum(lp[2] * lp[3])) + lambda_init
    nb = s // Q_BLOCK
    q_blocks = jnp.transpose(q, (0, 2, 3, 1, 4)).reshape(b, DIFF_HEADS, 2, nb, Q_BLOCK, d)
    q_blocks = jnp.moveaxis(q_blocks, 3, 0)
    starts = jnp.arange(nb, dtype=jnp.int32) * Q_BLOCK
    k_pos = jnp.arange(s, dtype=jnp.int32)

    def block(args):
        qb, start = args
        scores = jnp.einsum('bhcqd,bhckd->bhcqk', qb, k_sh)
        mask = k_pos[None, :] <= (start + jnp.arange(Q_BLOCK, dtype=jnp.int32))[:, None]
        p = jax.nn.softmax(jnp.where(mask, scores, -jnp.inf), axis=-1)
        attn = p[:, :, 0] - lam * p[:, :, 1]
        return jnp.einsum('bhqk,bhkv->bhqv', attn, v_sh)

    o = lax.map(block, (q_blocks, starts))
    o = jnp.moveaxis(o, 0, 2).reshape(b, DIFF_HEADS, s, 2 * d)
    o = jnp.moveaxis(o, 1, 2)
    o = rmsnorm(o, subln_g) * (1.0 - lambda_init)
    return o.reshape(b, s, DIFF_V_WIDTH).astype(h.dtype) @ w_out


def peer_ffn(h, w_q, sub_keys, u_tab, v_tab):
    b, s, dm = h.shape
    q = (h @ w_q).reshape(b, s, PEER_HEADS, 2, PEER_HALF).astype(jnp.float32)
    sc = jnp.einsum('bspjd,jnd->bspjn', q, sub_keys.astype(jnp.float32))
    s1, i1 = lax.top_k(sc[..., 0, :], PEER_TOPK)
    s2, i2 = lax.top_k(sc[..., 1, :], PEER_TOPK)
    n_cand = PEER_TOPK * PEER_TOPK
    cand = (s1[..., :, None] + s2[..., None, :]).reshape(b, s, PEER_HEADS, n_cand)
    cidx = (i1[..., :, None] * N_KEYS + i2[..., None, :]).reshape(b, s, PEER_HEADS, n_cand)
    best, pos = lax.top_k(cand, PEER_TOPK)
    idx = jnp.take_along_axis(cidx, pos, axis=-1)
    gate = jax.nn.softmax(best, axis=-1)
    n_sel = PEER_HEADS * PEER_TOPK
    nblk = (b * s) // PEER_TOKEN_BLOCK
    hb = h.reshape(nblk, PEER_TOKEN_BLOCK, dm)
    ib = idx.reshape(nblk, PEER_TOKEN_BLOCK, n_sel)
    gb = gate.reshape(nblk, PEER_TOKEN_BLOCK, n_sel).astype(h.dtype)

    def block(args):
        hx, ix, gx = args
        act = jax.nn.gelu(jnp.einsum('td,ted->te', hx, u_tab[ix]), approximate=False)
        return jnp.einsum('te,ted->td', gx * act, v_tab[ix])

    out = lax.map(block, (hb, ib, gb))
    return out.reshape(b, s, dm)


def setup_inputs(seed: int = 0) -> dict:
    key = jax.random.key(seed)
    ks = jax.random.split(key, 32)
    f32 = jnp.float32

    def nrm(k, shape, scale):
        return jax.random.normal(k, shape, f32) * scale

    def gain(k, shape):
        return 1.0 + 0.02 * jax.random.normal(k, shape, f32)

    D = D_MODEL
    x = nrm(ks[0], (BATCH, SEQ, D), 1.0)
    c = nrm(ks[1], (BATCH, D), 1.0)
    positions = (jax.random.randint(ks[2], (BATCH, 1), 0, 4096, dtype=jnp.int32)
                 + jnp.arange(SEQ, dtype=jnp.int32)[None, :])
    ada_w = nrm(ks[3], (DEPTH, D, 6 * D), 0.5 * D ** -0.5)
    ada_b = nrm(ks[4], (DEPTH, 6 * D), 0.02)
    norm_mix_g = gain(ks[5], (DEPTH, D))
    norm_ffn_g = gain(ks[6], (DEPTH, D))
    gdn_w_in = nrm(ks[7], (N_A_LAYERS, D, GDN_IN_WIDTH), D ** -0.5)
    gdn_conv_w = nrm(ks[8], (N_A_LAYERS, CONV_K, 3 * GDN_WIDTH), CONV_K ** -0.5)
    gdn_a_log = jnp.log(jax.random.uniform(ks[9], (N_A_LAYERS, GDN_HEADS), f32, 1.0, 16.0))
    dt = jnp.exp(jax.random.uniform(ks[10], (N_A_LAYERS, GDN_HEADS), f32, math.log(0.001), math.log(0.1)))
    gdn_dt_bias = dt + jnp.log(-jnp.expm1(-dt))
    gdn_o_gain = gain(ks[11], (N_A_LAYERS, GDN_HEAD_DIM))
    gdn_w_out = nrm(ks[12], (N_A_LAYERS, GDN_WIDTH, D), GDN_WIDTH ** -0.5)
    kv_norm_g = gain(ks[13], (D,))
    kv_ada_w = nrm(ks[14], (D, 2 * D), 0.5 * D ** -0.5)
    kv_ada_b = nrm(ks[15], (2 * D,), 0.02)
    kv_w = nrm(ks[16], (D, DIFF_Q_WIDTH + DIFF_V_WIDTH), D ** -0.5)
    diff_w_q = nrm(ks[17], (N_B_LAYERS, D, DIFF_Q_WIDTH), D ** -0.5)
    diff_lambda = nrm(ks[18], (N_B_LAYERS, 4, DIFF_HEAD_DIM), 0.1)
    diff_subln_g = gain(ks[19], (N_B_LAYERS, 2 * DIFF_HEAD_DIM))
    diff_w_out = nrm(ks[20], (N_B_LAYERS, DIFF_V_WIDTH, D), DIFF_V_WIDTH ** -0.5)
    peer_w_q = nrm(ks[21], (DEPTH, D, PEER_HEADS * PEER_QUERY_DIM), D ** -0.5)
    peer_sub_keys = nrm(ks[22], (DEPTH, 2, N_KEYS, PEER_HALF), PEER_HALF ** -0.5)
    peer_u = nrm(ks[23], (DEPTH, N_EXPERTS, D), D ** -0.5)
    peer_v = nrm(ks[24], (DEPTH, N_EXPERTS, D), PEER_HEADS ** -0.5)
    final_g = gain(ks[25], (D,))
    return {'x': x, 'c': c, 'positions': positions,
            'ada_w': ada_w, 'ada_b': ada_b, 'norm_mix_g': norm_mix_g, 'norm_ffn_g': norm_ffn_g,
            'gdn_w_in': gdn_w_in, 'gdn_conv_w': gdn_conv_w, 'gdn_a_log': gdn_a_log,
            'gdn_dt_bias': gdn_dt_bias, 'gdn_o_gain': gdn_o_gain, 'gdn_w_out': gdn_w_out,
            'kv_norm_g': kv_norm_g, 'kv_ada_w': kv_ada_w, 'kv_ada_b': kv_ada_b, 'kv_w': kv_w,
            'diff_w_q': diff_w_q, 'diff_lambda': diff_lambda, 'diff_subln_g': diff_subln_g,
            'diff_w_out': diff_w_out,
            'peer_w_q': peer_w_q, 'peer_sub_keys': peer_sub_keys, 'peer_u': peer_u, 'peer_v': peer_v,
            'final_g': final_g}


def reference(x, c, positions, ada_w, ada_b, norm_mix_g, norm_ffn_g,
              gdn_w_in, gdn_conv_w, gdn_a_log, gdn_dt_bias, gdn_o_gain, gdn_w_out,
              kv_norm_g, kv_ada_w, kv_ada_b, kv_w,
              diff_w_q, diff_lambda, diff_subln_g, diff_w_out,
              peer_w_q, peer_sub_keys, peer_u, peer_v, final_g):
    cos, sin = rope_tables(positions)
    k_sh, v_sh = None, None
    for l in range(DEPTH):
        sh1, sc1, gt1, sh2, sc2, gt2 = jnp.split(adaln(c, ada_w[l], ada_b[l]), 6, axis=-1)
        h = modulate(rmsnorm(x, norm_mix_g[l]), sh1, sc1)
        if l < N_A_LAYERS:
            y = gated_deltanet(h, gdn_w_in[l], gdn_conv_w[l], gdn_a_log[l], gdn_dt_bias[l],
                               gdn_o_gain[l], gdn_w_out[l])
        else:
            j = l - N_A_LAYERS
            lambda_init = 0.8 - 0.6 * math.exp(-0.3 * l)
            y = diff_attention(h, k_sh, v_sh, cos, sin, diff_w_q[j], diff_lambda[j],
                               diff_subln_g[j], diff_w_out[j], lambda_init)
        x = x + gt1[:, None, :] * y
        h = modulate(rmsnorm(x, norm_ffn_g[l]), sh2, sc2)
        x = x + gt2[:, None, :] * peer_ffn(h, peer_w_q[l], peer_sub_keys[l], peer_u[l], peer_v[l])
        if l == N_A_LAYERS - 1:
            k_sh, v_sh = shared_kv(x, c, kv_norm_g, kv_ada_w, kv_ada_b, kv_w, cos, sin)
    return rmsnorm(x, final_g)
```

```python
import functools
import math

import jax
import jax.numpy as jnp
from jax import lax
from jax.experimental import pallas as pl
from jax.experimental.pallas import tpu as pltpu

F32 = jnp.float32
BF16 = jnp.bfloat16

NORM_EPS = 1e-6
N_A_LAYERS = 2
GDN_HEADS = 16
GDN_HEAD_DIM = 128
CONV_K = 4
CHUNK = 64
DIFF_HEADS = 8
DIFF_HEAD_DIM = 128
ROPE_THETA = 10000.0
PEER_HEADS = 8
PEER_HALF = 128
N_KEYS = 128
PEER_TOPK = 16

LANES = 128
SUBLANES = 8
VMEM_LIMIT = 56 * 1024 * 1024


def _cparams(sem):
    return pltpu.CompilerParams(dimension_semantics=sem, vmem_limit_bytes=VMEM_LIMIT)


def _dot(a, b):
    return jnp.dot(a.astype(BF16), b.astype(BF16), preferred_element_type=F32)


def _dot_nt(a, b):
    return lax.dot_general(a.astype(BF16), b.astype(BF16), (((1,), (1,)), ((), ())),
                           preferred_element_type=F32)


def _dot_tn(a, b):
    return lax.dot_general(a.astype(BF16), b.astype(BF16), (((0,), (0,)), ((), ())),
                           preferred_element_type=F32)


def _split3(x):
    hi = x.astype(BF16)
    r1 = x - hi.astype(F32)
    mid = r1.astype(BF16)
    lo = (r1 - mid.astype(F32)).astype(BF16)
    return hi, mid, lo


def _sigmoid(x):
    return 1.0 / (1.0 + jnp.exp(-x))


def _softplus(x):
    return jnp.maximum(x, 0.0) + jnp.log(1.0 + jnp.exp(-jnp.abs(x)))


def _adaln_kernel(c_ref, w_ref, b_ref, o_ref):
    c = c_ref[...]
    s = c * _sigmoid(c)
    o_ref[0] = _dot(s, w_ref[0]) + b_ref[0]


def adaln_all(c_pad, w, b, tn=1024):
    nl, d, n = w.shape
    rows = c_pad.shape[0]
    return pl.pallas_call(
        _adaln_kernel,
        grid=(nl, n // tn),
        in_specs=[pl.BlockSpec((rows, d), lambda l, j: (0, 0)),
                  pl.BlockSpec((1, d, tn), lambda l, j: (l, 0, j)),
                  pl.BlockSpec((1, 1, tn), lambda l, j: (l, 0, j))],
        out_specs=pl.BlockSpec((1, rows, tn), lambda l, j: (l, 0, j)),
        out_shape=jax.ShapeDtypeStruct((nl, rows, n), F32),
        compiler_params=_cparams(("arbitrary", "arbitrary")),
        name="adaln",
    )(c_pad, w, b.reshape(nl, 1, n))


def _rope_tile(acc, cosf, sins, qscale):
    pieces = []
    for g in range(acc.shape[1] // LANES):
        xg = acc[:, g * LANES:(g + 1) * LANES]
        rot = pltpu.roll(xg, LANES // 2, axis=1)
        pieces.append((xg * cosf + rot * sins) * qscale)
    return jnp.concatenate(pieces, axis=1) if len(pieces) > 1 else pieces[0]


def _mm_kernel(*refs, has_norm, mode, emit_h, rope_tiles, qscale):
    it = iter(refs)
    x_ref = next(it)
    if has_norm:
        g_ref, sh_ref, sc_ref = next(it), next(it), next(it)
    w_ref = next(it)
    if mode == "rope":
        cos_ref, sin_ref = next(it), next(it)
    if mode == "res":
        res_ref, gate_ref = next(it), next(it)
    o_ref = next(it)
    if emit_h:
        hout_ref = next(it)
    if has_norm:
        h_scr = next(it)
    j = pl.program_id(1)

    if has_norm:
        @pl.when(j == 0)
        def _():
            x = x_ref[...]
            ms = jnp.mean(x * x, axis=-1, keepdims=True)
            y = x * lax.rsqrt(ms + NORM_EPS) * g_ref[...]
            h = (y * (1.0 + sc_ref[0]) + sh_ref[0]).astype(BF16)
            h_scr[...] = h
            if emit_h:
                hout_ref[...] = h
        lhs = h_scr[...]
    else:
        lhs = x_ref[...]

    acc = jnp.dot(lhs, w_ref[...], preferred_element_type=F32)
    if mode == "rope":
        @pl.when(j < rope_tiles)
        def _():
            o_ref[...] = _rope_tile(acc, cos_ref[...], sin_ref[...], qscale).astype(o_ref.dtype)

        @pl.when(j >= rope_tiles)
        def _():
            o_ref[...] = acc.astype(o_ref.dtype)
    elif mode == "res":
        o_ref[...] = (res_ref[...] + gate_ref[0] * acc).astype(o_ref.dtype)
    else:
        o_ref[...] = acc.astype(o_ref.dtype)


def fused_matmul(x, w, *, seq, norm=None, rope=None, res=None, out_dtype=F32, emit_h=False,
                 tm=512, tn=512):
    t, k = x.shape
    n = w.shape[1]
    tn = min(tn, n)
    assert t % tm == 0 and n % tn == 0 and seq % tm == 0
    bidx = lambda i: (i * tm) // seq
    in_specs = [pl.BlockSpec((tm, k), lambda i, j: (i, 0))]
    args = [x]
    if norm is not None:
        g, sh, sc = norm
        in_specs += [pl.BlockSpec((1, k), lambda i, j: (0, 0)),
                     pl.BlockSpec((1, 1, k), lambda i, j: (bidx(i), 0, 0)),
                     pl.BlockSpec((1, 1, k), lambda i, j: (bidx(i), 0, 0))]
        args += [g, sh, sc]
    in_specs.append(pl.BlockSpec((k, tn), lambda i, j: (0, j)))
    args.append(w)
    mode, rope_tiles, qscale = "none", 0, 1.0
    if rope is not None:
        cosf, sins, rope_cols, qscale = rope
        mode, rope_tiles = "rope", rope_cols // tn
        in_specs += [pl.BlockSpec((tm, LANES), lambda i, j: (i, 0)),
                     pl.BlockSpec((tm, LANES), lambda i, j: (i, 0))]
        args += [cosf, sins]
    if res is not None:
        r, gate = res
        mode = "res"
        in_specs += [pl.BlockSpec((tm, tn), lambda i, j: (i, j)),
                     pl.BlockSpec((1, 1, tn), lambda i, j: (bidx(i), 0, j))]
        args += [r, gate]
    out_specs = [pl.BlockSpec((tm, tn), lambda i, j: (i, j))]
    out_shape = [jax.ShapeDtypeStruct((t, n), out_dtype)]
    if emit_h:
        out_specs.append(pl.BlockSpec((tm, k), lambda i, j: (i, 0)))
        out_shape.append(jax.ShapeDtypeStruct((t, k), BF16))
    scratch = [pltpu.VMEM((tm, k), BF16)] if norm is not None else []
    outs = pl.pallas_call(
        functools.partial(_mm_kernel, has_norm=norm is not None, mode=mode, emit_h=emit_h,
                          rope_tiles=rope_tiles, qscale=qscale),
        grid=(t // tm, n // tn),
        in_specs=in_specs, out_specs=out_specs, out_shape=out_shape,
        scratch_shapes=scratch,
        compiler_params=_cparams(("arbitrary", "arbitrary")),
        name="fused_mm_" + mode,
    )(*args)
    return outs if emit_h else outs[0]


GDN_ROWS = 256
GDN_W = GDN_HEADS * GDN_HEAD_DIM


def _gdn_kernel(qkv_ref, z_ref, a_ref, b_ref, at_ref, convw_ref, alog_ref, dtb_ref, alogc_ref,
                dtbc_ref, ogain_ref, o_ref, xbuf, qkvs, state):
    r = pl.program_id(1)
    rows = GDN_ROWS
    hd = GDN_HEAD_DIM

    @pl.when(r == 0)
    def _():
        state[...] = jnp.zeros_like(state)
        xbuf[0:SUBLANES, :] = jnp.zeros((SUBLANES, 3 * GDN_W), F32)

    cb = 512
    for c in range(3 * GDN_W // cb):
        cs = slice(c * cb, (c + 1) * cb)
        xbuf[SUBLANES:SUBLANES + rows, cs] = qkv_ref[:, cs]
        acc = convw_ref[3:4, cs] * xbuf[SUBLANES:SUBLANES + rows, cs]
        for i in range(CONV_K - 1):
            off = SUBLANES - (CONV_K - 1) + i
            acc = acc + convw_ref[i:i + 1, cs] * xbuf[off:off + rows, cs]
        qkvs[:, cs] = acc * _sigmoid(acc)
        xbuf[0:SUBLANES, cs] = qkv_ref[rows - SUBLANES:rows, cs]

    neg_a = -jnp.exp(alog_ref[...])
    g_all = neg_a * _softplus(a_ref[...] + dtb_ref[...])
    beta_all = _sigmoid(b_ref[...])
    neg_ac = -jnp.exp(alogc_ref[...])

    ii = lax.broadcasted_iota(jnp.int32, (CHUNK, CHUNK), 0)
    jj = lax.broadcasted_iota(jnp.int32, (CHUNK, CHUNK), 1)
    strict = ii > jj
    eye = ii == jj
    tril = jnp.where(ii >= jj, 1.0, 0.0).astype(BF16)
    triu = jnp.where(ii <= jj, 1.0, 0.0).astype(BF16)
    eye_f = jnp.where(eye, 1.0, 0.0).astype(F32)
    sib_masks = []
    for m in range(CHUNK.bit_length() - 1):
        sib_masks.append(((ii >> (m + 1)) == (jj >> (m + 1))) & ((ii >> m) != (jj >> m)) & strict)
    ogain = ogain_ref[...]

    for ci in range(rows // CHUNK):
        r0 = ci * CHUNK
        g = g_all[r0:r0 + CHUNK]
        beta = beta_all[r0:r0 + CHUNK]
        gt = neg_ac * _softplus(at_ref[0, ci] + dtbc_ref[...])
        G = sum(jnp.dot(tril, p, preferred_element_type=F32) for p in _split3(g))
        GT = sum(jnp.dot(p, triu, preferred_element_type=F32) for p in _split3(gt))
        eG = jnp.exp(G)
        eGl = jnp.exp(G[CHUNK - 1:CHUNK, :] - G)
        glast = jnp.exp(G[CHUNK - 1:CHUNK, :])
        for h in range(GDN_HEADS):
            q = qkvs[r0:r0 + CHUNK, h * hd:(h + 1) * hd]
            k = qkvs[r0:r0 + CHUNK, GDN_W + h * hd:GDN_W + (h + 1) * hd]
            v = qkvs[r0:r0 + CHUNK, 2 * GDN_W + h * hd:2 * GDN_W + (h + 1) * hd]
            q = q * lax.rsqrt(jnp.sum(q * q, axis=-1, keepdims=True) + NORM_EPS) * (hd ** -0.5)
            k = k * lax.rsqrt(jnp.sum(k * k, axis=-1, keepdims=True) + NORM_EPS)
            beta_c = beta[:, h:h + 1]
            eG_c = eG[:, h:h + 1]
            kb = k * beta_c
            diff = G[:, h:h + 1] - GT[h:h + 1, :]
            dec = jnp.exp(jnp.where(strict, diff, -jnp.inf))
            lmat = _dot_nt(kb, k) * dec
            a_qk = _dot_nt(q, k) * jnp.where(eye, 1.0, dec)
            tinv = eye_f - jnp.where(sib_masks[0], lmat, 0.0)
            for sm in sib_masks[1:]:
                tinv = tinv - _dot(tinv, _dot(jnp.where(sm, lmat, 0.0), tinv))
            wu = _dot(tinv, jnp.concatenate([kb * eG_c, v * beta_c], axis=1))
            w_c, u_c = wu[:, :hd], wu[:, hd:]
            s = state[h]
            u_new = u_c - _dot(w_c, s)
            o = _dot(q * eG_c, s) + _dot(a_qk, u_new)
            state[h] = glast[:, h:h + 1] * s + _dot_tn(k * eGl[:, h:h + 1], u_new)
            o = o * lax.rsqrt(jnp.mean(o * o, axis=-1, keepdims=True) + NORM_EPS) * ogain
            zh = z_ref[r0:r0 + CHUNK, h * hd:(h + 1) * hd]
            o_ref[r0:r0 + CHUNK, h * hd:(h + 1) * hd] = (o * (zh * _sigmoid(zh))).astype(o_ref.dtype)


def gdn_core(proj, a, b, conv_w, a_log, dt_bias, o_gain, *, batch, seq):
    t = proj.shape[0]
    rows = GDN_ROWS
    nr = seq // rows
    cpr = rows // CHUNK
    h = GDN_HEADS
    a_t = a.reshape(batch, seq // CHUNK, CHUNK, h).transpose(0, 1, 3, 2)
    a_t = a_t.reshape(batch * nr, cpr, h, CHUNK)
    row_map = lambda bi, r: (bi * nr + r, 0)
    full = lambda bi, r: (0, 0)
    return pl.pallas_call(
        _gdn_kernel,
        grid=(batch, nr),
        in_specs=[pl.BlockSpec((rows, 3 * GDN_W), row_map),
                  pl.BlockSpec((rows, GDN_W), lambda bi, r: (bi * nr + r, 3)),
                  pl.BlockSpec((rows, h), row_map),
                  pl.BlockSpec((rows, h), row_map),
                  pl.BlockSpec((1, cpr, h, CHUNK), lambda bi, r: (bi * nr + r, 0, 0, 0)),
                  pl.BlockSpec((CONV_K, 3 * GDN_W), full),
                  pl.BlockSpec((1, h), full), pl.BlockSpec((1, h), full),
                  pl.BlockSpec((h, 1), full), pl.BlockSpec((h, 1), full),
                  pl.BlockSpec((1, GDN_HEAD_DIM), full)],
        out_specs=pl.BlockSpec((rows, GDN_W), row_map),
        out_shape=jax.ShapeDtypeStruct((t, GDN_W), BF16),
        scratch_shapes=[pltpu.VMEM((SUBLANES + rows, 3 * GDN_W), F32),
                        pltpu.VMEM((rows, 3 * GDN_W), F32),
                        pltpu.VMEM((h, GDN_HEAD_DIM, GDN_HEAD_DIM), F32)],
        compiler_params=_cparams(("arbitrary", "arbitrary")),
        name="gdn_core",
    )(proj, proj, a, b, a_t, conv_w, a_log.reshape(1, h), dt_bias.reshape(1, h),
      a_log.reshape(h, 1), dt_bias.reshape(h, 1), o_gain.reshape(1, GDN_HEAD_DIM))


ATT_TQ = 512
ATT_TK = 512


def _attn_kernel(lam_ref, q_ref, k_ref, v_ref, g_ref, o_ref, m_scr, l_scr, acc_scr, *, out_scale):
    i = pl.program_id(2)
    j = pl.program_id(3)
    d = DIFF_HEAD_DIM

    @pl.when(j == 0)
    def _():
        m_scr[...] = jnp.full(m_scr.shape, -jnp.inf, F32)
        l_scr[...] = jnp.zeros(l_scr.shape, F32)
        acc_scr[...] = jnp.zeros(acc_scr.shape, F32)

    @pl.when(j <= i)
    def _():
        q = q_ref[...]
        k = k_ref[...]
        v = v_ref[...]
        row = i * ATT_TQ + lax.broadcasted_iota(jnp.int32, (ATT_TQ, ATT_TK), 0)
        col = j * ATT_TK + lax.broadcasted_iota(jnp.int32, (ATT_TQ, ATT_TK), 1)
        mask = col <= row
        for c in range(2):
            s = _dot_nt(q[:, c * d:(c + 1) * d], k[:, c * d:(c + 1) * d])
            s = jnp.where(mask, s, -jnp.inf)
            m_prev = m_scr[c]
            m_new = jnp.maximum(m_prev, jnp.max(s, axis=1, keepdims=True))
            alpha = jnp.exp(m_prev - m_new)
            p = jnp.exp(s - m_new)
            l_scr[c] = alpha * l_scr[c] + jnp.sum(p, axis=1, keepdims=True)
            acc_scr[c] = alpha * acc_scr[c] + _dot(p, v)
            m_scr[c] = m_new

    @pl.when(j == i)
    def _():
        lam = lam_ref[0]
        o = acc_scr[0] / l_scr[0] - lam * (acc_scr[1] / l_scr[1])
        o = o * lax.rsqrt(jnp.mean(o * o, axis=-1, keepdims=True) + NORM_EPS) * g_ref[...]
        o_ref[...] = (o * out_scale).astype(o_ref.dtype)


def diff_attn_core(q, kv, lam, subln_g, *, batch, seq, out_scale):
    t, width = q.shape
    hw = 2 * DIFF_HEAD_DIM
    nq, nk = seq // ATT_TQ, seq // ATT_TK
    return pl.pallas_call(
        functools.partial(_attn_kernel, out_scale=out_scale),
        grid=(batch, DIFF_HEADS, nq, nk),
        in_specs=[pl.BlockSpec(memory_space=pltpu.SMEM),
                  pl.BlockSpec((ATT_TQ, hw), lambda b, h, i, j: (b * nq + i, h)),
                  pl.BlockSpec((ATT_TK, hw), lambda b, h, i, j: (b * nk + jnp.minimum(j, i), h)),
                  pl.BlockSpec((ATT_TK, hw),
                               lambda b, h, i, j: (b * nk + jnp.minimum(j, i), DIFF_HEADS + h)),
                  pl.BlockSpec((1, hw), lambda b, h, i, j: (0, 0))],
        out_specs=pl.BlockSpec((ATT_TQ, hw), lambda b, h, i, j: (b * nq + i, h)),
        out_shape=jax.ShapeDtypeStruct((t, width), BF16),
        scratch_shapes=[pltpu.VMEM((2, ATT_TQ, 1), F32), pltpu.VMEM((2, ATT_TQ, 1), F32),
                        pltpu.VMEM((2, ATT_TQ, hw), F32)],
        compiler_params=_cparams(("arbitrary", "arbitrary", "arbitrary", "arbitrary")),
        name="diff_attn",
    )(lam, q, kv, kv, subln_g.reshape(1, hw))


PEER_TT = 512
PEER_TE = 512
_CAND_COUNTS = tuple(min(PEER_TOPK, (PEER_TOPK + 1) // (i + 1)) for i in range(PEER_TOPK))
_N_CAND = sum(_CAND_COUNTS)
_N_CAND_PAD = -(-_N_CAND // SUBLANES) * SUBLANES


def _extract_top(x, n):
    rows = x.shape[0]
    iota = lax.broadcasted_iota(jnp.int32, x.shape, 0)
    vals = []
    for _ in range(n):
        m = jnp.max(x, axis=0, keepdims=True)
        idx = jnp.min(jnp.where(x == m, iota, rows), axis=0, keepdims=True)
        x = jnp.where(iota == idx, -jnp.inf, x)
        vals.append(m)
    return vals


def _peer_topk_kernel(q_ref, keys_ref, s1_ref, s2_ref, tau_ref):
    tt = q_ref.shape[0]
    kparts = [_split3(keys_ref[j])[:2] for j in range(2)]
    for p in range(PEER_HEADS):
        sc = []
        for j in range(2):
            c0 = (2 * p + j) * PEER_HALF
            qh, ql = _split3(q_ref[:, c0:c0 + PEER_HALF])[:2]
            kh, kl = kparts[j]
            nt = lambda a, b: lax.dot_general(a, b, (((1,), (1,)), ((), ())),
                                              preferred_element_type=F32)
            sc.append(nt(kh, qh) + (nt(kh, ql) + nt(kl, qh)))
        v1 = _extract_top(sc[0], PEER_TOPK)
        v2 = jnp.concatenate(_extract_top(sc[1], PEER_TOPK), axis=0)
        cand = [v1[i] + v2[0:_CAND_COUNTS[i]] for i in range(PEER_TOPK)]
        if _N_CAND_PAD > _N_CAND:
            cand.append(jnp.full((_N_CAND_PAD - _N_CAND, tt), -jnp.inf, F32))
        best = _extract_top(jnp.concatenate(cand, axis=0), PEER_TOPK + 1)
        zsum = sum(jnp.exp(bk - best[0]) for bk in best[:PEER_TOPK])
        mz = best[0] + jnp.log(zsum)
        tau = 0.5 * (best[PEER_TOPK - 1] + best[PEER_TOPK])
        s1_ref[0, p] = sc[0] - mz
        s2_ref[0, p] = sc[1]
        tau_ref[0, p:p + 1, :] = tau - mz


def peer_topk(q, sub_keys):
    t = q.shape[0]
    tt = PEER_TT
    nt = t // tt
    big = jax.ShapeDtypeStruct((nt, PEER_HEADS, N_KEYS, tt), F32)
    return pl.pallas_call(
        _peer_topk_kernel,
        grid=(nt,),
        in_specs=[pl.BlockSpec((tt, q.shape[1]), lambda i: (i, 0)),
                  pl.BlockSpec((2, N_KEYS, PEER_HALF), lambda i: (0, 0, 0))],
        out_specs=[pl.BlockSpec((1, PEER_HEADS, N_KEYS, tt), lambda i: (i, 0, 0, 0)),
                   pl.BlockSpec((1, PEER_HEADS, N_KEYS, tt), lambda i: (i, 0, 0, 0)),
                   pl.BlockSpec((1, PEER_HEADS, tt), lambda i: (i, 0, 0))],
        out_shape=[big, big, jax.ShapeDtypeStruct((nt, PEER_HEADS, tt), F32)],
        compiler_params=_cparams(("arbitrary",)),
        name="peer_topk",
    )(q, sub_keys)


def _peer_expert_kernel(h_ref, s1_ref, s2_ref, tau_ref, u_ref, vt_ref, x_ref, gate_ref, o_ref, acc):
    e = pl.program_id(1)
    ne = pl.num_programs(1)
    tt = h_ref.shape[0]
    a_per = PEER_TE // N_KEYS

    @pl.when(e == 0)
    def _():
        acc[...] = jnp.zeros_like(acc)

    act = lax.dot_general(u_ref[...], h_ref[...], (((1,), (1,)), ((), ())),
                          preferred_element_type=F32)
    gel = 0.5 * act * (1.0 + lax.erf(act * (2.0 ** -0.5)))
    pieces = []
    for ai in range(a_per):
        a = e * a_per + ai
        gsum = jnp.zeros((N_KEYS, tt), F32)
        for p in range(PEER_HEADS):
            tsum = s1_ref[0, p, pl.ds(a, 1), :] + s2_ref[0, p]
            gsum = gsum + jnp.where(tsum > tau_ref[0, p:p + 1, :], jnp.exp(tsum), 0.0)
        pieces.append(gsum * gel[ai * N_KEYS:(ai + 1) * N_KEYS])
    wgt = jnp.concatenate(pieces, axis=0).astype(BF16)
    acc[...] += jnp.dot(vt_ref[...], wgt, preferred_element_type=F32)

    @pl.when(e == ne - 1)
    def _():
        o_ref[...] = x_ref[...] + gate_ref[0] * acc[...].T


def peer_experts(h, s1, s2, tau, u_bf, vt_bf, x, gate, *, seq):
    t, d = h.shape
    n_exp = u_bf.shape[0]
    tt, te = PEER_TT, PEER_TE
    bidx = lambda i: (i * tt) // seq
    return pl.pallas_call(
        _peer_expert_kernel,
        grid=(t // tt, n_exp // te),
        in_specs=[pl.BlockSpec((tt, d), lambda i, e: (i, 0)),
                  pl.BlockSpec((1, PEER_HEADS, N_KEYS, tt), lambda i, e: (i, 0, 0, 0)),
                  pl.BlockSpec((1, PEER_HEADS, N_KEYS, tt), lambda i, e: (i, 0, 0, 0)),
                  pl.BlockSpec((1, PEER_HEADS, tt), lambda i, e: (i, 0, 0)),
                  pl.BlockSpec((te, d), lambda i, e: (e, 0)),
                  pl.BlockSpec((d, te), lambda i, e: (0, e)),
                  pl.BlockSpec((tt, d), lambda i, e: (i, 0)),
                  pl.BlockSpec((1, 1, d), lambda i, e: (bidx(i), 0, 0))],
        out_specs=pl.BlockSpec((tt, d), lambda i, e: (i, 0)),
        out_shape=jax.ShapeDtypeStruct((t, d), F32),
        scratch_shapes=[pltpu.VMEM((d, tt), F32)],
        compiler_params=_cparams(("arbitrary", "arbitrary")),
        name="peer_experts",
    )(h, s1, s2, tau, u_bf, vt_bf, x, gate)


def _rmsnorm_kernel(x_ref, g_ref, o_ref):
    x = x_ref[...]
    o_ref[...] = x * lax.rsqrt(jnp.mean(x * x, axis=-1, keepdims=True) + NORM_EPS) * g_ref[...]


def final_rmsnorm(x, g, tm=512):
    t, d = x.shape
    return pl.pallas_call(
        _rmsnorm_kernel,
        grid=(t // tm,),
        in_specs=[pl.BlockSpec((tm, d), lambda i: (i, 0)), pl.BlockSpec((1, d), lambda i: (0, 0))],
        out_specs=pl.BlockSpec((tm, d), lambda i: (i, 0)),
        out_shape=jax.ShapeDtypeStruct((t, d), F32),
        compiler_params=_cparams(("arbitrary",)),
        name="final_rmsnorm",
    )(x, g.reshape(1, d))


def _rope_tables(positions):
    d = DIFF_HEAD_DIM
    inv_freq = ROPE_THETA ** (-jnp.arange(0, d, 2, dtype=F32) / d)
    ang = positions.astype(F32)[..., None] * inv_freq
    cos, sin = jnp.cos(ang), jnp.sin(ang)
    cosf = jnp.concatenate([cos, cos], axis=-1).reshape(-1, d)
    sins = jnp.concatenate([-sin, sin], axis=-1).reshape(-1, d)
    return cosf, sins


def kernel(x, c, positions, ada_w, ada_b, norm_mix_g, norm_ffn_g, gdn_w_in, gdn_conv_w, gdn_a_log,
           gdn_dt_bias, gdn_o_gain, gdn_w_out, kv_norm_g, kv_ada_w, kv_ada_b, kv_w, diff_w_q,
           diff_lambda, diff_subln_g, diff_w_out, peer_w_q, peer_sub_keys, peer_u, peer_v, final_g):
    batch, seq, d = x.shape
    depth = ada_w.shape[0]
    t = batch * seq
    xs = x.reshape(t, d)
    cosf, sins = _rope_tables(positions)

    c_pad = jnp.pad(c, ((0, SUBLANES - batch), (0, 0)))
    mods = adaln_all(c_pad, ada_w, ada_b)[:, :batch]
    kv_mods = adaln_all(c_pad, kv_ada_w[None], kv_ada_b[None])[0, :batch]
    mod = lambda m: m[:, None, :]

    kv_sh = None
    for l in range(depth):
        sh1, sc1, gt1, sh2, sc2, gt2 = [mod(m) for m in jnp.split(mods[l], 6, axis=-1)]
        norm1 = (norm_mix_g[l].reshape(1, d), sh1, sc1)
        if l < N_A_LAYERS:
            w_in = gdn_w_in[l]
            w4 = 4 * GDN_W
            proj = fused_matmul(xs, w_in[:, :w4].astype(BF16), seq=seq, norm=norm1, emit_h=True)
            proj, h1 = proj
            w_ab = jnp.pad(w_in[:, w4:], ((0, 0), (0, LANES - 2 * GDN_HEADS))).astype(BF16)
            ab = fused_matmul(h1, w_ab, seq=seq)
            o = gdn_core(proj, ab[:, :GDN_HEADS],
                         ab[:, GDN_HEADS:2 * GDN_HEADS], gdn_conv_w[l], gdn_a_log[l],
                         gdn_dt_bias[l], gdn_o_gain[l], batch=batch, seq=seq)
            xs = fused_matmul(o, gdn_w_out[l].astype(BF16), seq=seq, res=(xs, gt1))
        else:
            j = l - N_A_LAYERS
            lambda_init = 0.8 - 0.6 * math.exp(-0.3 * l)
            lp = diff_lambda[j].astype(F32)
            lam = jnp.exp(jnp.sum(lp[0] * lp[1])) - jnp.exp(jnp.sum(lp[2] * lp[3])) + lambda_init
            qw = DIFF_HEADS * 2 * DIFF_HEAD_DIM
            q = fused_matmul(xs, diff_w_q[j].astype(BF16), seq=seq, norm=norm1,
                             rope=(cosf, sins, qw, DIFF_HEAD_DIM ** -0.5), out_dtype=BF16)
            o = diff_attn_core(q, kv_sh, lam.reshape(1), diff_subln_g[j], batch=batch,
                               seq=seq, out_scale=1.0 - lambda_init)
            xs = fused_matmul(o, diff_w_out[j].astype(BF16), seq=seq, res=(xs, gt1))

        norm2 = (norm_ffn_g[l].reshape(1, d), sh2, sc2)
        pq, h2 = fused_matmul(xs, peer_w_q[l].astype(BF16), seq=seq, norm=norm2, emit_h=True)
        s1, s2, tau = peer_topk(pq, peer_sub_keys[l])
        xs = peer_experts(h2, s1, s2, tau, peer_u[l].astype(BF16), peer_v[l].T.astype(BF16),
                          xs, gt2, seq=seq)

        if l == N_A_LAYERS - 1:
            kvsh, kvsc = [mod(m) for m in jnp.split(kv_mods, 2, axis=-1)]
            qw = DIFF_HEADS * 2 * DIFF_HEAD_DIM
            kv_sh = fused_matmul(xs, kv_w.astype(BF16), seq=seq,
                                 norm=(kv_norm_g.reshape(1, d), kvsh, kvsc),
                                 rope=(cosf, sins, qw, 1.0), out_dtype=BF16)

    return final_rmsnorm(xs, final_g).reshape(batch, seq, d)
```

```python
import functools
import math

import jax
import jax.numpy as jnp
from jax import lax
from jax.experimental import pallas as pl
from jax.experimental.pallas import tpu as pltpu

F32 = jnp.float32
BF16 = jnp.bfloat16

NORM_EPS = 1e-6
N_A_LAYERS = 2
GDN_HEADS = 16
GDN_HEAD_DIM = 128
CONV_K = 4
CHUNK = 64
DIFF_HEADS = 8
DIFF_HEAD_DIM = 128
ROPE_THETA = 10000.0
PEER_HEADS = 8
PEER_HALF = 128
N_KEYS = 128
PEER_TOPK = 16

LOG2E = math.log2(math.e)

LANES = 128
SUBLANES = 8
VMEM_LIMIT = 56 * 1024 * 1024


def _cparams(sem):
    return pltpu.CompilerParams(dimension_semantics=sem, vmem_limit_bytes=VMEM_LIMIT)


def _dot(a, b):
    return jnp.dot(a.astype(BF16), b.astype(BF16), preferred_element_type=F32)


def _dot_nt(a, b):
    return lax.dot_general(a.astype(BF16), b.astype(BF16), (((1,), (1,)), ((), ())),
                           preferred_element_type=F32)


def _dot_tn(a, b):
    return lax.dot_general(a.astype(BF16), b.astype(BF16), (((0,), (0,)), ((), ())),
                           preferred_element_type=F32)


def _split3(x):
    hi = x.astype(BF16)
    r1 = x - hi.astype(F32)
    mid = r1.astype(BF16)
    lo = (r1 - mid.astype(F32)).astype(BF16)
    return hi, mid, lo


def _sigmoid(x):
    return 1.0 / (1.0 + jnp.exp(-x))


def _softplus(x):
    return jnp.maximum(x, 0.0) + jnp.log(1.0 + jnp.exp(-jnp.abs(x)))


def _adaln_kernel(c_ref, w_ref, b_ref, o_ref):
    c = c_ref[...]
    s = c * _sigmoid(c)
    o_ref[0] = _dot(s, w_ref[0]) + b_ref[0]


def adaln_all(c_pad, w, b, tn=1024):
    nl, d, n = w.shape
    rows = c_pad.shape[0]
    return pl.pallas_call(
        _adaln_kernel,
        grid=(nl, n // tn),
        in_specs=[pl.BlockSpec((rows, d), lambda l, j: (0, 0)),
                  pl.BlockSpec((1, d, tn), lambda l, j: (l, 0, j)),
                  pl.BlockSpec((1, 1, tn), lambda l, j: (l, 0, j))],
        out_specs=pl.BlockSpec((1, rows, tn), lambda l, j: (l, 0, j)),
        out_shape=jax.ShapeDtypeStruct((nl, rows, n), F32),
        compiler_params=_cparams(("arbitrary", "arbitrary")),
        name="adaln",
    )(c_pad, w, b.reshape(nl, 1, n))


def _rope_tile(acc, cosf, sins, qscale):
    pieces = []
    for g in range(acc.shape[1] // LANES):
        xg = acc[:, g * LANES:(g + 1) * LANES]
        rot = pltpu.roll(xg, LANES // 2, axis=1)
        pieces.append((xg * cosf + rot * sins) * qscale)
    return jnp.concatenate(pieces, axis=1) if len(pieces) > 1 else pieces[0]


def _mm_kernel(*refs, has_norm, mode, emit_h, rope_tiles, qscale):
    it = iter(refs)
    x_ref = next(it)
    if has_norm:
        g_ref, sh_ref, sc_ref = next(it), next(it), next(it)
    w_ref = next(it)
    if mode == "rope":
        cos_ref, sin_ref = next(it), next(it)
    if mode == "res":
        res_ref, gate_ref = next(it), next(it)
    o_ref = next(it)
    if emit_h:
        hout_ref = next(it)
    if has_norm:
        h_scr = next(it)
    j = pl.program_id(1)

    if has_norm:
        @pl.when(j == 0)
        def _():
            x = x_ref[...]
            ms = jnp.mean(x * x, axis=-1, keepdims=True)
            y = x * lax.rsqrt(ms + NORM_EPS) * g_ref[...]
            h = (y * (1.0 + sc_ref[0]) + sh_ref[0]).astype(BF16)
            h_scr[...] = h
            if emit_h:
                hout_ref[...] = h
        lhs = h_scr[...]
    else:
        lhs = x_ref[...]

    acc = jnp.dot(lhs, w_ref[...], preferred_element_type=F32)
    if mode == "rope":
        @pl.when(j < rope_tiles)
        def _():
            o_ref[...] = _rope_tile(acc, cos_ref[...], sin_ref[...], qscale).astype(o_ref.dtype)

        @pl.when(j >= rope_tiles)
        def _():
            o_ref[...] = acc.astype(o_ref.dtype)
    elif mode == "res":
        o_ref[...] = (res_ref[...] + gate_ref[0] * acc).astype(o_ref.dtype)
    else:
        o_ref[...] = acc.astype(o_ref.dtype)


def fused_matmul(x, w, *, seq, norm=None, rope=None, res=None, out_dtype=F32, emit_h=False,
                 tm=512, tn=512):
    t, k = x.shape
    n = w.shape[1]
    tn = min(tn, n)
    assert t % tm == 0 and n % tn == 0 and seq % tm == 0
    bidx = lambda i: (i * tm) // seq
    in_specs = [pl.BlockSpec((tm, k), lambda i, j: (i, 0))]
    args = [x]
    if norm is not None:
        g, sh, sc = norm
        in_specs += [pl.BlockSpec((1, k), lambda i, j: (0, 0)),
                     pl.BlockSpec((1, 1, k), lambda i, j: (bidx(i), 0, 0)),
                     pl.BlockSpec((1, 1, k), lambda i, j: (bidx(i), 0, 0))]
        args += [g, sh, sc]
    in_specs.append(pl.BlockSpec((k, tn), lambda i, j: (0, j)))
    args.append(w)
    mode, rope_tiles, qscale = "none", 0, 1.0
    if rope is not None:
        cosf, sins, rope_cols, qscale = rope
        mode, rope_tiles = "rope", rope_cols // tn
        in_specs += [pl.BlockSpec((tm, LANES), lambda i, j: (i, 0)),
                     pl.BlockSpec((tm, LANES), lambda i, j: (i, 0))]
        args += [cosf, sins]
    if res is not None:
        r, gate = res
        mode = "res"
        in_specs += [pl.BlockSpec((tm, tn), lambda i, j: (i, j)),
                     pl.BlockSpec((1, 1, tn), lambda i, j: (bidx(i), 0, j))]
        args += [r, gate]
    out_specs = [pl.BlockSpec((tm, tn), lambda i, j: (i, j))]
    out_shape = [jax.ShapeDtypeStruct((t, n), out_dtype)]
    if emit_h:
        out_specs.append(pl.BlockSpec((tm, k), lambda i, j: (i, 0)))
        out_shape.append(jax.ShapeDtypeStruct((t, k), BF16))
    scratch = [pltpu.VMEM((tm, k), BF16)] if norm is not None else []
    outs = pl.pallas_call(
        functools.partial(_mm_kernel, has_norm=norm is not None, mode=mode, emit_h=emit_h,
                          rope_tiles=rope_tiles, qscale=qscale),
        grid=(t // tm, n // tn),
        in_specs=in_specs, out_specs=out_specs, out_shape=out_shape,
        scratch_shapes=scratch,
        compiler_params=_cparams(("arbitrary", "arbitrary")),
        name="fused_mm_" + mode,
    )(*args)
    return outs if emit_h else outs[0]


GDN_ROWS = 256
GDN_W = GDN_HEADS * GDN_HEAD_DIM


def _gdn_kernel(qkv_ref, z_ref, a_ref, b_ref, at_ref, convw_ref, alog_ref, dtb_ref, alogc_ref,
                dtbc_ref, ogain_ref, o_ref, xbuf, qkvs, qn_s, kn_s, kb_s, qg_s, kd_s, kbgv_s, dec_s,
                l_s, a_s, tinv_s, tinvb_s, t1_s, w_s, u_s, unew_s, state, stateb):
    r = pl.program_id(1)
    rows = GDN_ROWS
    hd = GDN_HEAD_DIM
    nh = GDN_HEADS
    ncs = rows // CHUNK
    units = [(ci, h) for ci in range(ncs) for h in range(nh)]
    rsl = lambda ci: slice(ci * CHUNK, (ci + 1) * CHUNK)
    csl = lambda h: slice(h * hd, (h + 1) * hd)

    @pl.when(r == 0)
    def _():
        state[...] = jnp.zeros_like(state)
        stateb[...] = jnp.zeros_like(stateb)
        xbuf[0:SUBLANES, :] = jnp.zeros((SUBLANES, 3 * GDN_W), F32)

    cb = 512
    for c in range(3 * GDN_W // cb):
        cs = slice(c * cb, (c + 1) * cb)
        xbuf[SUBLANES:SUBLANES + rows, cs] = qkv_ref[:, cs]
        acc = convw_ref[3:4, cs] * xbuf[SUBLANES:SUBLANES + rows, cs]
        for i in range(CONV_K - 1):
            off = SUBLANES - (CONV_K - 1) + i
            acc = acc + convw_ref[i:i + 1, cs] * xbuf[off:off + rows, cs]
        qkvs[:, cs] = acc * _sigmoid(acc)
        xbuf[0:SUBLANES, cs] = qkv_ref[rows - SUBLANES:rows, cs]

    neg_a = -jnp.exp(alog_ref[...])
    g_all = neg_a * _softplus(a_ref[...] + dtb_ref[...])
    beta_all = _sigmoid(b_ref[...])
    neg_ac = -jnp.exp(alogc_ref[...])

    ii = lax.broadcasted_iota(jnp.int32, (CHUNK, CHUNK), 0)
    jj = lax.broadcasted_iota(jnp.int32, (CHUNK, CHUNK), 1)
    strict = ii > jj
    eye = ii == jj
    tril = jnp.where(ii >= jj, 1.0, 0.0).astype(BF16)
    triu = jnp.where(ii <= jj, 1.0, 0.0).astype(BF16)
    eye_f = jnp.where(eye, 1.0, 0.0).astype(F32)
    sib_masks = []
    for m in range(CHUNK.bit_length() - 1):
        sib_masks.append(((ii >> (m + 1)) == (jj >> (m + 1))) & ((ii >> m) != (jj >> m)) & strict)
    ogain = ogain_ref[...]

    glast = []
    for ci in range(ncs):
        rs = rsl(ci)
        g = g_all[rs]
        beta = beta_all[rs]
        gt = neg_ac * _softplus(at_ref[0, ci] + dtbc_ref[...])
        G = sum(jnp.dot(tril, p, preferred_element_type=F32) for p in _split3(g))
        GT = sum(jnp.dot(p, triu, preferred_element_type=F32) for p in _split3(gt))
        eG = jnp.exp(G)
        eGl = jnp.exp(G[CHUNK - 1:CHUNK, :] - G)
        glast.append(jnp.exp(G[CHUNK - 1:CHUNK, :]))
        for h in range(nh):
            cs = csl(h)
            q = qkvs[rs, cs]
            k = qkvs[rs, GDN_W + h * hd:GDN_W + (h + 1) * hd]
            v = qkvs[rs, 2 * GDN_W + h * hd:2 * GDN_W + (h + 1) * hd]
            q = q * lax.rsqrt(jnp.sum(q * q, axis=-1, keepdims=True) + NORM_EPS) * (hd ** -0.5)
            k = k * lax.rsqrt(jnp.sum(k * k, axis=-1, keepdims=True) + NORM_EPS)
            beta_c = beta[:, h:h + 1]
            eG_c = eG[:, h:h + 1]
            kb = k * beta_c
            qn_s[rs, cs] = q.astype(BF16)
            kn_s[rs, cs] = k.astype(BF16)
            kb_s[rs, cs] = kb.astype(BF16)
            qg_s[rs, cs] = (q * eG_c).astype(BF16)
            kd_s[rs, cs] = (k * eGl[:, h:h + 1]).astype(BF16)
            kbgv_s[rs, 2 * h * hd:(2 * h + 1) * hd] = (kb * eG_c).astype(BF16)
            kbgv_s[rs, (2 * h + 1) * hd:(2 * h + 2) * hd] = (v * beta_c).astype(BF16)
            diff = G[:, h:h + 1] - GT[h:h + 1, :]
            dec_s[ci * nh + h] = jnp.exp(jnp.where(strict, diff, -jnp.inf))

    nt = lambda a, b: lax.dot_general(a, b, (((1,), (1,)), ((), ())), preferred_element_type=F32)
    mm = lambda a, b: jnp.dot(a, b, preferred_element_type=F32)

    for u, (ci, h) in enumerate(units):
        rs, cs = rsl(ci), csl(h)
        dec = dec_s[u]
        kn = kn_s[rs, cs]
        lmat = nt(kb_s[rs, cs], kn) * dec
        l_s[u] = lmat
        t0 = eye_f - jnp.where(sib_masks[0], lmat, 0.0)
        tinv_s[u] = t0
        tinvb_s[u] = t0.astype(BF16)
        a_s[u] = (nt(qn_s[rs, cs], kn) * jnp.where(eye, 1.0, dec)).astype(BF16)

    for sm in sib_masks[1:]:
        for u in range(len(units)):
            t1_s[u] = mm(jnp.where(sm, l_s[u], 0.0).astype(BF16), tinvb_s[u]).astype(BF16)
        for u in range(len(units)):
            tn = tinv_s[u] - mm(tinvb_s[u], t1_s[u])
            tinv_s[u] = tn
            tinvb_s[u] = tn.astype(BF16)

    for u, (ci, h) in enumerate(units):
        rs, cs = rsl(ci), csl(h)
        wu = mm(tinvb_s[u], kbgv_s[rs, 2 * h * hd:(2 * h + 2) * hd])
        w_s[rs, cs] = wu[:, :hd].astype(BF16)
        u_s[rs, cs] = wu[:, hd:]

    for ci in range(ncs):
        rs = rsl(ci)
        for h in range(nh):
            cs = csl(h)
            unew_s[h] = (u_s[rs, cs] - mm(w_s[rs, cs], stateb[h])).astype(BF16)
        for h in range(nh):
            cs = csl(h)
            un = unew_s[h]
            o = mm(qg_s[rs, cs], stateb[h]) + mm(a_s[ci * nh + h], un)
            snew = glast[ci][:, h:h + 1] * state[h] + lax.dot_general(
                kd_s[rs, cs], un, (((0,), (0,)), ((), ())), preferred_element_type=F32)
            state[h] = snew
            stateb[h] = snew.astype(BF16)
            o = o * lax.rsqrt(jnp.mean(o * o, axis=-1, keepdims=True) + NORM_EPS) * ogain
            zh = z_ref[rs, cs]
            o_ref[rs, cs] = (o * (zh * _sigmoid(zh))).astype(o_ref.dtype)


def gdn_core(proj, a, b, conv_w, a_log, dt_bias, o_gain, *, batch, seq):
    t = proj.shape[0]
    rows = GDN_ROWS
    nr = seq // rows
    cpr = rows // CHUNK
    h = GDN_HEADS
    a_t = a.reshape(batch, seq // CHUNK, CHUNK, h).transpose(0, 1, 3, 2)
    a_t = a_t.reshape(batch * nr, cpr, h, CHUNK)
    row_map = lambda bi, r: (bi * nr + r, 0)
    full = lambda bi, r: (0, 0)
    return pl.pallas_call(
        _gdn_kernel,
        grid=(batch, nr),
        in_specs=[pl.BlockSpec((rows, 3 * GDN_W), row_map),
                  pl.BlockSpec((rows, GDN_W), lambda bi, r: (bi * nr + r, 3)),
                  pl.BlockSpec((rows, h), row_map),
                  pl.BlockSpec((rows, h), row_map),
                  pl.BlockSpec((1, cpr, h, CHUNK), lambda bi, r: (bi * nr + r, 0, 0, 0)),
                  pl.BlockSpec((CONV_K, 3 * GDN_W), full),
                  pl.BlockSpec((1, h), full), pl.BlockSpec((1, h), full),
                  pl.BlockSpec((h, 1), full), pl.BlockSpec((h, 1), full),
                  pl.BlockSpec((1, GDN_HEAD_DIM), full)],
        out_specs=pl.BlockSpec((rows, GDN_W), row_map),
        out_shape=jax.ShapeDtypeStruct((t, GDN_W), BF16),
        scratch_shapes=[pltpu.VMEM((SUBLANES + rows, 3 * GDN_W), F32),
                        pltpu.VMEM((rows, 3 * GDN_W), F32),
                        pltpu.VMEM((rows, GDN_W), BF16),
                        pltpu.VMEM((rows, GDN_W), BF16),
                        pltpu.VMEM((rows, GDN_W), BF16),
                        pltpu.VMEM((rows, GDN_W), BF16),
                        pltpu.VMEM((rows, GDN_W), BF16),
                        pltpu.VMEM((rows, 2 * GDN_W), BF16),
                        pltpu.VMEM((cpr * h, CHUNK, CHUNK), F32),
                        pltpu.VMEM((cpr * h, CHUNK, CHUNK), F32),
                        pltpu.VMEM((cpr * h, CHUNK, CHUNK), BF16),
                        pltpu.VMEM((cpr * h, CHUNK, CHUNK), F32),
                        pltpu.VMEM((cpr * h, CHUNK, CHUNK), BF16),
                        pltpu.VMEM((cpr * h, CHUNK, CHUNK), BF16),
                        pltpu.VMEM((rows, GDN_W), BF16),
                        pltpu.VMEM((rows, GDN_W), F32),
                        pltpu.VMEM((h, CHUNK, GDN_HEAD_DIM), BF16),
                        pltpu.VMEM((h, GDN_HEAD_DIM, GDN_HEAD_DIM), F32),
                        pltpu.VMEM((h, GDN_HEAD_DIM, GDN_HEAD_DIM), BF16)],
        compiler_params=_cparams(("arbitrary", "arbitrary")),
        name="gdn_core",
    )(proj, proj, a, b, a_t, conv_w, a_log.reshape(1, h), dt_bias.reshape(1, h),
      a_log.reshape(h, 1), dt_bias.reshape(h, 1), o_gain.reshape(1, GDN_HEAD_DIM))


ATT_TQ = 512
ATT_TK = 512


def _attn_kernel(lam_ref, q_ref, k_ref, vt_ref, g_ref, o_ref, m_scr, l_scr, acc_scr, *, out_scale):
    i = pl.program_id(2)
    d = DIFF_HEAD_DIM
    tq, tk = ATT_TQ, ATT_TK
    q = q_ref[...]
    qs = [q[:, c * d:(c + 1) * d] for c in range(2)]
    m_scr[...] = jnp.full(m_scr.shape, -jnp.inf, F32)
    l_scr[...] = jnp.zeros(l_scr.shape, F32)
    acc_scr[...] = jnp.zeros(acc_scr.shape, F32)

    def tile(j, masked):
        k = k_ref[pl.ds(pl.multiple_of(j * tk, tk), tk), :]
        vt = vt_ref[0, 0, j]
        for c in range(2):
            st = lax.dot_general(k[:, c * d:(c + 1) * d], qs[c], (((1,), (1,)), ((), ())),
                                 preferred_element_type=F32)
            if masked:
                kpos = lax.broadcasted_iota(jnp.int32, (tk, tq), 0)
                qpos = lax.broadcasted_iota(jnp.int32, (tk, tq), 1)
                st = jnp.where(kpos <= qpos, st, -jnp.inf)
            m_prev = m_scr[c]
            m_new = jnp.maximum(m_prev, jnp.max(st, axis=0, keepdims=True))
            alpha = jnp.exp2(m_prev - m_new)
            p = jnp.exp2(st - m_new)
            l_scr[c] = alpha * l_scr[c] + jnp.sum(p, axis=0, keepdims=True)
            acc_scr[c] = alpha * acc_scr[c] + jnp.dot(vt, p.astype(BF16),
                                                      preferred_element_type=F32)
            m_scr[c] = m_new

    def body(j, carry):
        tile(j, False)
        return carry

    lax.fori_loop(0, i, body, 0)
    tile(i, True)

    lam = lam_ref[0]
    o = acc_scr[0] / l_scr[0] - lam * (acc_scr[1] / l_scr[1])
    o = o * lax.rsqrt(jnp.mean(o * o, axis=0, keepdims=True) + NORM_EPS) * (g_ref[...] * out_scale)
    o_ref[...] = o.T.astype(o_ref.dtype)


def diff_attn_core(q, kv, lam, subln_g, *, batch, seq, out_scale):
    t, width = q.shape
    hw = 2 * DIFF_HEAD_DIM
    assert ATT_TQ == ATT_TK
    nq, nk = seq // ATT_TQ, seq // ATT_TK
    vt = kv[:, width:].reshape(batch, nk, ATT_TK, DIFF_HEADS, hw).transpose(0, 3, 1, 4, 2)
    return pl.pallas_call(
        functools.partial(_attn_kernel, out_scale=out_scale),
        grid=(batch, DIFF_HEADS, nq),
        in_specs=[pl.BlockSpec(memory_space=pltpu.SMEM),
                  pl.BlockSpec((ATT_TQ, hw), lambda b, h, i: (b * nq + i, h)),
                  pl.BlockSpec((seq, hw), lambda b, h, i: (b, h)),
                  pl.BlockSpec((1, 1, nk, hw, ATT_TK), lambda b, h, i: (b, h, 0, 0, 0)),
                  pl.BlockSpec((hw, 1), lambda b, h, i: (0, 0))],
        out_specs=pl.BlockSpec((ATT_TQ, hw), lambda b, h, i: (b * nq + i, h)),
        out_shape=jax.ShapeDtypeStruct((t, width), BF16),
        scratch_shapes=[pltpu.VMEM((2, 1, ATT_TQ), F32), pltpu.VMEM((2, 1, ATT_TQ), F32),
                        pltpu.VMEM((2, hw, ATT_TQ), F32)],
        compiler_params=_cparams(("arbitrary", "arbitrary", "arbitrary")),
        name="diff_attn",
    )(lam, q, kv, vt, subln_g.reshape(hw, 1))


PEER_TT = 512
PEER_TE = 1024
PEER_SUB = 256
_CAND_COUNTS = tuple(min(PEER_TOPK, (PEER_TOPK + 1) // (i + 1)) for i in range(PEER_TOPK))
_N_CAND = sum(_CAND_COUNTS)
_N_CAND_PAD = -(-_N_CAND // SUBLANES) * SUBLANES


def _extract_top(x, n):
    rows = x.shape[0]
    iota = lax.broadcasted_iota(jnp.int32, x.shape, 0)
    vals = []
    for _ in range(n):
        m = jnp.max(x, axis=0, keepdims=True)
        idx = jnp.min(jnp.where(x == m, iota, rows), axis=0, keepdims=True)
        x = jnp.where(iota == idx, -jnp.inf, x)
        vals.append(m)
    return vals


def _peer_topk_kernel(q_ref, keys_ref, s1_ref, s2_ref, tau_ref):
    tt = q_ref.shape[0]
    kparts = [_split3(keys_ref[j])[:2] for j in range(2)]
    for p in range(PEER_HEADS):
        sc = []
        for j in range(2):
            c0 = (2 * p + j) * PEER_HALF
            qh, ql = _split3(q_ref[:, c0:c0 + PEER_HALF])[:2]
            kh, kl = kparts[j]
            nt = lambda a, b: lax.dot_general(a, b, (((1,), (1,)), ((), ())),
                                              preferred_element_type=F32)
            sc.append(nt(kh, qh) + (nt(kh, ql) + nt(kl, qh)))
        v1 = _extract_top(sc[0], PEER_TOPK)
        v2 = jnp.concatenate(_extract_top(sc[1], PEER_TOPK), axis=0)
        cand = [v1[i] + v2[0:_CAND_COUNTS[i]] for i in range(PEER_TOPK)]
        if _N_CAND_PAD > _N_CAND:
            cand.append(jnp.full((_N_CAND_PAD - _N_CAND, tt), -jnp.inf, F32))
        best = _extract_top(jnp.concatenate(cand, axis=0), PEER_TOPK + 1)
        zsum = sum(jnp.exp(bk - best[0]) for bk in best[:PEER_TOPK])
        mz = best[0] + jnp.log(zsum)
        tau = 0.5 * (best[PEER_TOPK - 1] + best[PEER_TOPK])
        s1_ref[0, p] = (sc[0] - mz) * LOG2E
        s2_ref[0, p] = sc[1] * LOG2E
        tau_ref[0, p:p + 1, :] = (tau - mz) * LOG2E


def peer_topk(q, sub_keys):
    t = q.shape[0]
    tt = PEER_TT
    nt = t // tt
    big = jax.ShapeDtypeStruct((nt, PEER_HEADS, N_KEYS, tt), F32)
    return pl.pallas_call(
        _peer_topk_kernel,
        grid=(nt,),
        in_specs=[pl.BlockSpec((tt, q.shape[1]), lambda i: (i, 0)),
                  pl.BlockSpec((2, N_KEYS, PEER_HALF), lambda i: (0, 0, 0))],
        out_specs=[pl.BlockSpec((1, PEER_HEADS, N_KEYS, tt), lambda i: (i, 0, 0, 0)),
                   pl.BlockSpec((1, PEER_HEADS, N_KEYS, tt), lambda i: (i, 0, 0, 0)),
                   pl.BlockSpec((1, PEER_HEADS, tt), lambda i: (i, 0, 0))],
        out_shape=[big, big, jax.ShapeDtypeStruct((nt, PEER_HEADS, tt), F32)],
        compiler_params=_cparams(("arbitrary",)),
        name="peer_topk",
    )(q, sub_keys)


def _peer_expert_kernel(h_ref, s1_ref, s2_ref, tau_ref, u_ref, vt_ref, x_ref, gate_ref, o_ref, acc):
    e = pl.program_id(1)
    ne = pl.num_programs(1)
    tt = h_ref.shape[0]
    a_per_sub = PEER_SUB // N_KEYS
    h = h_ref[...]

    @pl.when(e == 0)
    def _():
        acc[...] = jnp.zeros_like(acc)

    total = None
    for sq in range(PEER_TE // PEER_SUB):
        es = slice(sq * PEER_SUB, (sq + 1) * PEER_SUB)
        act = lax.dot_general(u_ref[es, :], h, (((1,), (1,)), ((), ())),
                              preferred_element_type=F32)
        gel = 0.5 * act * (1.0 + lax.erf(act * (2.0 ** -0.5)))
        pieces = []
        for ai in range(a_per_sub):
            a = e * (PEER_TE // N_KEYS) + sq * a_per_sub + ai
            gsum = jnp.zeros((N_KEYS, tt), F32)
            for p in range(PEER_HEADS):
                tsum = s1_ref[0, p, pl.ds(a, 1), :] + s2_ref[0, p]
                gsum = gsum + jnp.where(tsum > tau_ref[0, p:p + 1, :], jnp.exp2(tsum), 0.0)
            pieces.append(gsum * gel[ai * N_KEYS:(ai + 1) * N_KEYS])
        wgt = jnp.concatenate(pieces, axis=0).astype(BF16)
        part = jnp.dot(vt_ref[:, es], wgt, preferred_element_type=F32)
        total = part if total is None else total + part
    acc[...] += total

    @pl.when(e == ne - 1)
    def _():
        o_ref[...] = x_ref[...] + gate_ref[0] * acc[...].T


def peer_experts(h, s1, s2, tau, u_bf, vt_bf, x, gate, *, seq):
    t, d = h.shape
    n_exp = u_bf.shape[0]
    tt, te = PEER_TT, PEER_TE
    bidx = lambda i: (i * tt) // seq
    once = pl.Buffered(1)
    return pl.pallas_call(
        _peer_expert_kernel,
        grid=(t // tt, n_exp // te),
        in_specs=[pl.BlockSpec((tt, d), lambda i, e: (i, 0), pipeline_mode=once),
                  pl.BlockSpec((1, PEER_HEADS, N_KEYS, tt), lambda i, e: (i, 0, 0, 0),
                               pipeline_mode=once),
                  pl.BlockSpec((1, PEER_HEADS, N_KEYS, tt), lambda i, e: (i, 0, 0, 0),
                               pipeline_mode=once),
                  pl.BlockSpec((1, PEER_HEADS, tt), lambda i, e: (i, 0, 0)),
                  pl.BlockSpec((te, d), lambda i, e: (e, 0)),
                  pl.BlockSpec((d, te), lambda i, e: (0, e)),
                  pl.BlockSpec((tt, d), lambda i, e: (i, 0), pipeline_mode=once),
                  pl.BlockSpec((1, 1, d), lambda i, e: (bidx(i), 0, 0))],
        out_specs=pl.BlockSpec((tt, d), lambda i, e: (i, 0)),
        out_shape=jax.ShapeDtypeStruct((t, d), F32),
        scratch_shapes=[pltpu.VMEM((d, tt), F32)],
        compiler_params=_cparams(("arbitrary", "arbitrary")),
        name="peer_experts",
    )(h, s1, s2, tau, u_bf, vt_bf, x, gate)


def _rmsnorm_kernel(x_ref, g_ref, o_ref):
    x = x_ref[...]
    o_ref[...] = x * lax.rsqrt(jnp.mean(x * x, axis=-1, keepdims=True) + NORM_EPS) * g_ref[...]


def final_rmsnorm(x, g, tm=512):
    t, d = x.shape
    return pl.pallas_call(
        _rmsnorm_kernel,
        grid=(t // tm,),
        in_specs=[pl.BlockSpec((tm, d), lambda i: (i, 0)), pl.BlockSpec((1, d), lambda i: (0, 0))],
        out_specs=pl.BlockSpec((tm, d), lambda i: (i, 0)),
        out_shape=jax.ShapeDtypeStruct((t, d), F32),
        compiler_params=_cparams(("arbitrary",)),
        name="final_rmsnorm",
    )(x, g.reshape(1, d))


def _rope_tables(positions):
    d = DIFF_HEAD_DIM
    inv_freq = ROPE_THETA ** (-jnp.arange(0, d, 2, dtype=F32) / d)
    ang = positions.astype(F32)[..., None] * inv_freq
    cos, sin = jnp.cos(ang), jnp.sin(ang)
    cosf = jnp.concatenate([cos, cos], axis=-1).reshape(-1, d)
    sins = jnp.concatenate([-sin, sin], axis=-1).reshape(-1, d)
    return cosf, sins


def kernel(x, c, positions, ada_w, ada_b, norm_mix_g, norm_ffn_g, gdn_w_in, gdn_conv_w, gdn_a_log,
           gdn_dt_bias, gdn_o_gain, gdn_w_out, kv_norm_g, kv_ada_w, kv_ada_b, kv_w, diff_w_q,
           diff_lambda, diff_subln_g, diff_w_out, peer_w_q, peer_sub_keys, peer_u, peer_v, final_g):
    batch, seq, d = x.shape
    depth = ada_w.shape[0]
    t = batch * seq
    xs = x.reshape(t, d)
    cosf, sins = _rope_tables(positions)

    c_pad = jnp.pad(c, ((0, SUBLANES - batch), (0, 0)))
    mods = adaln_all(c_pad, ada_w, ada_b)[:, :batch]
    kv_mods = adaln_all(c_pad, kv_ada_w[None], kv_ada_b[None])[0, :batch]
    mod = lambda m: m[:, None, :]

    kv_sh = None
    for l in range(depth):
        sh1, sc1, gt1, sh2, sc2, gt2 = [mod(m) for m in jnp.split(mods[l], 6, axis=-1)]
        norm1 = (norm_mix_g[l].reshape(1, d), sh1, sc1)
        if l < N_A_LAYERS:
            w_in = gdn_w_in[l]
            w4 = 4 * GDN_W
            proj = fused_matmul(xs, w_in[:, :w4].astype(BF16), seq=seq, norm=norm1, emit_h=True)
            proj, h1 = proj
            w_ab = jnp.pad(w_in[:, w4:], ((0, 0), (0, LANES - 2 * GDN_HEADS))).astype(BF16)
            ab = fused_matmul(h1, w_ab, seq=seq)
            o = gdn_core(proj, ab[:, :GDN_HEADS],
                         ab[:, GDN_HEADS:2 * GDN_HEADS], gdn_conv_w[l], gdn_a_log[l],
                         gdn_dt_bias[l], gdn_o_gain[l], batch=batch, seq=seq)
            xs = fused_matmul(o, gdn_w_out[l].astype(BF16), seq=seq, res=(xs, gt1))
        else:
            j = l - N_A_LAYERS
            lambda_init = 0.8 - 0.6 * math.exp(-0.3 * l)
            lp = diff_lambda[j].astype(F32)
            lam = jnp.exp(jnp.sum(lp[0] * lp[1])) - jnp.exp(jnp.sum(lp[2] * lp[3])) + lambda_init
            qw = DIFF_HEADS * 2 * DIFF_HEAD_DIM
            q = fused_matmul(xs, diff_w_q[j].astype(BF16), seq=seq, norm=norm1,
                             rope=(cosf, sins, qw, DIFF_HEAD_DIM ** -0.5 * LOG2E), out_dtype=BF16)
            o = diff_attn_core(q, kv_sh, lam.reshape(1), diff_subln_g[j], batch=batch,
                               seq=seq, out_scale=1.0 - lambda_init)
            xs = fused_matmul(o, diff_w_out[j].astype(BF16), seq=seq, res=(xs, gt1))

        norm2 = (norm_ffn_g[l].reshape(1, d), sh2, sc2)
        pq, h2 = fused_matmul(xs, peer_w_q[l].astype(BF16), seq=seq, norm=norm2, emit_h=True)
        s1, s2, tau = peer_topk(pq, peer_sub_keys[l])
        xs = peer_experts(h2, s1, s2, tau, peer_u[l].astype(BF16), peer_v[l].T.astype(BF16),
                          xs, gt2, seq=seq)

        if l == N_A_LAYERS - 1:
            kvsh, kvsc = [mod(m) for m in jnp.split(kv_mods, 2, axis=-1)]
            qw = DIFF_HEADS * 2 * DIFF_HEAD_DIM
            kv_sh = fused_matmul(xs, kv_w.astype(BF16), seq=seq,
                                 norm=(kv_norm_g.reshape(1, d), kvsh, kvsc),
                                 rope=(cosf, sins, qw, 1.0), out_dtype=BF16)

    return final_rmsnorm(xs, final_g).reshape(batch, seq, d)
```

```python
import functools
import math

import jax
import jax.numpy as jnp
from jax import lax
from jax.experimental import pallas as pl
from jax.experimental.pallas import tpu as pltpu

F32 = jnp.float32
BF16 = jnp.bfloat16

NORM_EPS = 1e-6
N_A_LAYERS = 2
GDN_HEADS = 16
GDN_HEAD_DIM = 128
CONV_K = 4
CHUNK = 64
DIFF_HEADS = 8
DIFF_HEAD_DIM = 128
ROPE_THETA = 10000.0
PEER_HEADS = 8
PEER_HALF = 128
N_KEYS = 128
PEER_TOPK = 16

LOG2E = math.log2(math.e)

LANES = 128
SUBLANES = 8
VMEM_LIMIT = 56 * 1024 * 1024


def _cparams(sem):
    return pltpu.CompilerParams(dimension_semantics=sem, vmem_limit_bytes=VMEM_LIMIT)


def _dot(a, b):
    return jnp.dot(a.astype(BF16), b.astype(BF16), preferred_element_type=F32)


def _dot_nt(a, b):
    return lax.dot_general(a.astype(BF16), b.astype(BF16), (((1,), (1,)), ((), ())),
                           preferred_element_type=F32)


def _dot_tn(a, b):
    return lax.dot_general(a.astype(BF16), b.astype(BF16), (((0,), (0,)), ((), ())),
                           preferred_element_type=F32)


def _split3(x):
    hi = x.astype(BF16)
    r1 = x - hi.astype(F32)
    mid = r1.astype(BF16)
    lo = (r1 - mid.astype(F32)).astype(BF16)
    return hi, mid, lo


def _sigmoid(x):
    return 1.0 / (1.0 + jnp.exp(-x))


def _softplus(x):
    return jnp.maximum(x, 0.0) + jnp.log(1.0 + jnp.exp(-jnp.abs(x)))


def _adaln_kernel(c_ref, w_ref, b_ref, o_ref):
    c = c_ref[...]
    s = c * _sigmoid(c)
    o_ref[0] = _dot(s, w_ref[0]) + b_ref[0]


def adaln_all(c_pad, w, b, tn=1024):
    nl, d, n = w.shape
    rows = c_pad.shape[0]
    return pl.pallas_call(
        _adaln_kernel,
        grid=(nl, n // tn),
        in_specs=[pl.BlockSpec((rows, d), lambda l, j: (0, 0)),
                  pl.BlockSpec((1, d, tn), lambda l, j: (l, 0, j)),
                  pl.BlockSpec((1, 1, tn), lambda l, j: (l, 0, j))],
        out_specs=pl.BlockSpec((1, rows, tn), lambda l, j: (l, 0, j)),
        out_shape=jax.ShapeDtypeStruct((nl, rows, n), F32),
        compiler_params=_cparams(("arbitrary", "arbitrary")),
        name="adaln",
    )(c_pad, w, b.reshape(nl, 1, n))


def _rope_tile(acc, cosf, sins, qscale):
    pieces = []
    for g in range(acc.shape[1] // LANES):
        xg = acc[:, g * LANES:(g + 1) * LANES]
        rot = pltpu.roll(xg, LANES // 2, axis=1)
        pieces.append((xg * cosf + rot * sins) * qscale)
    return jnp.concatenate(pieces, axis=1) if len(pieces) > 1 else pieces[0]


def _mm_kernel(*refs, has_norm, mode, emit_h, rope_tiles, qscale):
    it = iter(refs)
    x_ref = next(it)
    if has_norm:
        g_ref, sh_ref, sc_ref = next(it), next(it), next(it)
    w_ref = next(it)
    if mode == "rope":
        cos_ref, sin_ref = next(it), next(it)
    if mode == "res":
        res_ref, gate_ref = next(it), next(it)
    o_ref = next(it)
    if emit_h:
        hout_ref = next(it)
    if has_norm:
        h_scr = next(it)
    j = pl.program_id(1)

    if has_norm:
        @pl.when(j == 0)
        def _():
            x = x_ref[...]
            ms = jnp.mean(x * x, axis=-1, keepdims=True)
            y = x * lax.rsqrt(ms + NORM_EPS) * g_ref[...]
            h = (y * (1.0 + sc_ref[0]) + sh_ref[0]).astype(BF16)
            h_scr[...] = h
            if emit_h:
                hout_ref[...] = h
        lhs = h_scr[...]
    else:
        lhs = x_ref[...]

    acc = jnp.dot(lhs, w_ref[...], preferred_element_type=F32)
    if mode == "rope":
        @pl.when(j < rope_tiles)
        def _():
            o_ref[...] = _rope_tile(acc, cos_ref[...], sin_ref[...], qscale).astype(o_ref.dtype)

        @pl.when(j >= rope_tiles)
        def _():
            o_ref[...] = acc.astype(o_ref.dtype)
    elif mode == "res":
        o_ref[...] = (res_ref[...] + gate_ref[0] * acc).astype(o_ref.dtype)
    else:
        o_ref[...] = acc.astype(o_ref.dtype)


def fused_matmul(x, w, *, seq, norm=None, rope=None, res=None, out_dtype=F32, emit_h=False,
                 tm=512, tn=512):
    t, k = x.shape
    n = w.shape[1]
    tn = min(tn, n)
    assert t % tm == 0 and n % tn == 0 and seq % tm == 0
    bidx = lambda i: (i * tm) // seq
    in_specs = [pl.BlockSpec((tm, k), lambda i, j: (i, 0))]
    args = [x]
    if norm is not None:
        g, sh, sc = norm
        in_specs += [pl.BlockSpec((1, k), lambda i, j: (0, 0)),
                     pl.BlockSpec((1, 1, k), lambda i, j: (bidx(i), 0, 0)),
                     pl.BlockSpec((1, 1, k), lambda i, j: (bidx(i), 0, 0))]
        args += [g, sh, sc]
    in_specs.append(pl.BlockSpec((k, tn), lambda i, j: (0, j)))
    args.append(w)
    mode, rope_tiles, qscale = "none", 0, 1.0
    if rope is not None:
        cosf, sins, rope_cols, qscale = rope
        mode, rope_tiles = "rope", rope_cols // tn
        in_specs += [pl.BlockSpec((tm, LANES), lambda i, j: (i, 0)),
                     pl.BlockSpec((tm, LANES), lambda i, j: (i, 0))]
        args += [cosf, sins]
    if res is not None:
        r, gate = res
        mode = "res"
        in_specs += [pl.BlockSpec((tm, tn), lambda i, j: (i, j)),
                     pl.BlockSpec((1, 1, tn), lambda i, j: (bidx(i), 0, j))]
        args += [r, gate]
    out_specs = [pl.BlockSpec((tm, tn), lambda i, j: (i, j))]
    out_shape = [jax.ShapeDtypeStruct((t, n), out_dtype)]
    if emit_h:
        out_specs.append(pl.BlockSpec((tm, k), lambda i, j: (i, 0)))
        out_shape.append(jax.ShapeDtypeStruct((t, k), BF16))
    scratch = [pltpu.VMEM((tm, k), BF16)] if norm is not None else []
    outs = pl.pallas_call(
        functools.partial(_mm_kernel, has_norm=norm is not None, mode=mode, emit_h=emit_h,
                          rope_tiles=rope_tiles, qscale=qscale),
        grid=(t // tm, n // tn),
        in_specs=in_specs, out_specs=out_specs, out_shape=out_shape,
        scratch_shapes=scratch,
        compiler_params=_cparams(("arbitrary", "arbitrary")),
        name="fused_mm_" + mode,
    )(*args)
    return outs if emit_h else outs[0]


GDN_ROWS = 256
GDN_W = GDN_HEADS * GDN_HEAD_DIM


def _gdn_kernel(qkv_ref, z_ref, a_ref, b_ref, at_ref, convw_ref, alog_ref, dtb_ref, alogc_ref,
                dtbc_ref, ogain_ref, o_ref, xbuf, qkvs, qn_s, kn_s, kb_s, qg_s, kd_s, kbgv_s, dec_s,
                l_s, a_s, tinv_s, tinvb_s, t1_s, w_s, u_s, unew_s, state, stateb):
    r = pl.program_id(1)
    rows = GDN_ROWS
    hd = GDN_HEAD_DIM
    nh = GDN_HEADS
    ncs = rows // CHUNK
    units = [(ci, h) for ci in range(ncs) for h in range(nh)]
    rsl = lambda ci: slice(ci * CHUNK, (ci + 1) * CHUNK)
    csl = lambda h: slice(h * hd, (h + 1) * hd)

    @pl.when(r == 0)
    def _():
        state[...] = jnp.zeros_like(state)
        stateb[...] = jnp.zeros_like(stateb)
        xbuf[0:SUBLANES, :] = jnp.zeros((SUBLANES, 3 * GDN_W), F32)

    cb = 512
    for c in range(3 * GDN_W // cb):
        cs = slice(c * cb, (c + 1) * cb)
        xbuf[SUBLANES:SUBLANES + rows, cs] = qkv_ref[:, cs]
        acc = convw_ref[3:4, cs] * xbuf[SUBLANES:SUBLANES + rows, cs]
        for i in range(CONV_K - 1):
            off = SUBLANES - (CONV_K - 1) + i
            acc = acc + convw_ref[i:i + 1, cs] * xbuf[off:off + rows, cs]
        qkvs[:, cs] = acc * _sigmoid(acc)
        xbuf[0:SUBLANES, cs] = qkv_ref[rows - SUBLANES:rows, cs]

    neg_a = -jnp.exp(alog_ref[...])
    g_all = neg_a * _softplus(a_ref[...] + dtb_ref[...])
    beta_all = _sigmoid(b_ref[...])
    neg_ac = -jnp.exp(alogc_ref[...])

    ii = lax.broadcasted_iota(jnp.int32, (CHUNK, CHUNK), 0)
    jj = lax.broadcasted_iota(jnp.int32, (CHUNK, CHUNK), 1)
    strict = ii > jj
    eye = ii == jj
    tril = jnp.where(ii >= jj, 1.0, 0.0).astype(BF16)
    triu = jnp.where(ii <= jj, 1.0, 0.0).astype(BF16)
    eye_f = jnp.where(eye, 1.0, 0.0).astype(F32)
    sib_masks = []
    for m in range(CHUNK.bit_length() - 1):
        sib_masks.append(((ii >> (m + 1)) == (jj >> (m + 1))) & ((ii >> m) != (jj >> m)) & strict)
    ogain = ogain_ref[...]

    glast = []
    for ci in range(ncs):
        rs = rsl(ci)
        g = g_all[rs]
        beta = beta_all[rs]
        gt = neg_ac * _softplus(at_ref[0, ci] + dtbc_ref[...])
        G = sum(jnp.dot(tril, p, preferred_element_type=F32) for p in _split3(g))
        GT = sum(jnp.dot(p, triu, preferred_element_type=F32) for p in _split3(gt))
        eG = jnp.exp(G)
        eGl = jnp.exp(G[CHUNK - 1:CHUNK, :] - G)
        glast.append(jnp.exp(G[CHUNK - 1:CHUNK, :]))
        for h in range(nh):
            cs = csl(h)
            q = qkvs[rs, cs]
            k = qkvs[rs, GDN_W + h * hd:GDN_W + (h + 1) * hd]
            v = qkvs[rs, 2 * GDN_W + h * hd:2 * GDN_W + (h + 1) * hd]
            q = q * lax.rsqrt(jnp.sum(q * q, axis=-1, keepdims=True) + NORM_EPS) * (hd ** -0.5)
            k = k * lax.rsqrt(jnp.sum(k * k, axis=-1, keepdims=True) + NORM_EPS)
            beta_c = beta[:, h:h + 1]
            eG_c = eG[:, h:h + 1]
            kb = k * beta_c
            qn_s[rs, cs] = q.astype(BF16)
            kn_s[rs, cs] = k.astype(BF16)
            kb_s[rs, cs] = kb.astype(BF16)
            qg_s[rs, cs] = (q * eG_c).astype(BF16)
            kd_s[rs, cs] = (k * eGl[:, h:h + 1]).astype(BF16)
            kbgv_s[rs, 2 * h * hd:(2 * h + 1) * hd] = (kb * eG_c).astype(BF16)
            kbgv_s[rs, (2 * h + 1) * hd:(2 * h + 2) * hd] = (v * beta_c).astype(BF16)
            diff = G[:, h:h + 1] - GT[h:h + 1, :]
            dec_s[ci * nh + h] = jnp.exp(jnp.where(strict, diff, -jnp.inf))

    nt = lambda a, b: lax.dot_general(a, b, (((1,), (1,)), ((), ())), preferred_element_type=F32)
    mm = lambda a, b: jnp.dot(a, b, preferred_element_type=F32)

    for u, (ci, h) in enumerate(units):
        rs, cs = rsl(ci), csl(h)
        dec = dec_s[u]
        kn = kn_s[rs, cs]
        lmat = nt(kb_s[rs, cs], kn) * dec
        l_s[u] = lmat
        t0 = eye_f - jnp.where(sib_masks[0], lmat, 0.0)
        tinv_s[u] = t0
        tinvb_s[u] = t0.astype(BF16)
        a_s[u] = (nt(qn_s[rs, cs], kn) * jnp.where(eye, 1.0, dec)).astype(BF16)

    for sm in sib_masks[1:]:
        for u in range(len(units)):
            t1_s[u] = mm(jnp.where(sm, l_s[u], 0.0).astype(BF16), tinvb_s[u]).astype(BF16)
        for u in range(len(units)):
            tn = tinv_s[u] - mm(tinvb_s[u], t1_s[u])
            tinv_s[u] = tn
            tinvb_s[u] = tn.astype(BF16)

    for u, (ci, h) in enumerate(units):
        rs, cs = rsl(ci), csl(h)
        wu = mm(tinvb_s[u], kbgv_s[rs, 2 * h * hd:(2 * h + 2) * hd])
        w_s[rs, cs] = wu[:, :hd].astype(BF16)
        u_s[rs, cs] = wu[:, hd:]

    for ci in range(ncs):
        rs = rsl(ci)
        for h in range(nh):
            cs = csl(h)
            unew_s[h] = (u_s[rs, cs] - mm(w_s[rs, cs], stateb[h])).astype(BF16)
        for h in range(nh):
            cs = csl(h)
            un = unew_s[h]
            o = mm(qg_s[rs, cs], stateb[h]) + mm(a_s[ci * nh + h], un)
            snew = glast[ci][:, h:h + 1] * state[h] + lax.dot_general(
                kd_s[rs, cs], un, (((0,), (0,)), ((), ())), preferred_element_type=F32)
            state[h] = snew
            stateb[h] = snew.astype(BF16)
            o = o * lax.rsqrt(jnp.mean(o * o, axis=-1, keepdims=True) + NORM_EPS) * ogain
            zh = z_ref[rs, cs]
            o_ref[rs, cs] = (o * (zh * _sigmoid(zh))).astype(o_ref.dtype)


def gdn_core(proj, a, b, conv_w, a_log, dt_bias, o_gain, *, batch, seq):
    t = proj.shape[0]
    rows = GDN_ROWS
    nr = seq // rows
    cpr = rows // CHUNK
    h = GDN_HEADS
    a_t = a.reshape(batch, seq // CHUNK, CHUNK, h).transpose(0, 1, 3, 2)
    a_t = a_t.reshape(batch * nr, cpr, h, CHUNK)
    row_map = lambda bi, r: (bi * nr + r, 0)
    full = lambda bi, r: (0, 0)
    return pl.pallas_call(
        _gdn_kernel,
        grid=(batch, nr),
        in_specs=[pl.BlockSpec((rows, 3 * GDN_W), row_map),
                  pl.BlockSpec((rows, GDN_W), lambda bi, r: (bi * nr + r, 3)),
                  pl.BlockSpec((rows, h), row_map),
                  pl.BlockSpec((rows, h), row_map),
                  pl.BlockSpec((1, cpr, h, CHUNK), lambda bi, r: (bi * nr + r, 0, 0, 0)),
                  pl.BlockSpec((CONV_K, 3 * GDN_W), full),
                  pl.BlockSpec((1, h), full), pl.BlockSpec((1, h), full),
                  pl.BlockSpec((h, 1), full), pl.BlockSpec((h, 1), full),
                  pl.BlockSpec((1, GDN_HEAD_DIM), full)],
        out_specs=pl.BlockSpec((rows, GDN_W), row_map),
        out_shape=jax.ShapeDtypeStruct((t, GDN_W), BF16),
        scratch_shapes=[pltpu.VMEM((SUBLANES + rows, 3 * GDN_W), F32),
                        pltpu.VMEM((rows, 3 * GDN_W), F32),
                        pltpu.VMEM((rows, GDN_W), BF16),
                        pltpu.VMEM((rows, GDN_W), BF16),
                        pltpu.VMEM((rows, GDN_W), BF16),
                        pltpu.VMEM((rows, GDN_W), BF16),
                        pltpu.VMEM((rows, GDN_W), BF16),
                        pltpu.VMEM((rows, 2 * GDN_W), BF16),
                        pltpu.VMEM((cpr * h, CHUNK, CHUNK), F32),
                        pltpu.VMEM((cpr * h, CHUNK, CHUNK), F32),
                        pltpu.VMEM((cpr * h, CHUNK, CHUNK), BF16),
                        pltpu.VMEM((cpr * h, CHUNK, CHUNK), F32),
                        pltpu.VMEM((cpr * h, CHUNK, CHUNK), BF16),
                        pltpu.VMEM((cpr * h, CHUNK, CHUNK), BF16),
                        pltpu.VMEM((rows, GDN_W), BF16),
                        pltpu.VMEM((rows, GDN_W), F32),
                        pltpu.VMEM((h, CHUNK, GDN_HEAD_DIM), BF16),
                        pltpu.VMEM((h, GDN_HEAD_DIM, GDN_HEAD_DIM), F32),
                        pltpu.VMEM((h, GDN_HEAD_DIM, GDN_HEAD_DIM), BF16)],
        compiler_params=_cparams(("arbitrary", "arbitrary")),
        name="gdn_core",
    )(proj, proj, a, b, a_t, conv_w, a_log.reshape(1, h), dt_bias.reshape(1, h),
      a_log.reshape(h, 1), dt_bias.reshape(h, 1), o_gain.reshape(1, GDN_HEAD_DIM))


ATT_TQ = 1024
ATT_TK = 1024


def _attn_kernel(lam_ref, q_ref, k_ref, vt_ref, g_ref, o_ref, m_scr, l_scr, acc_scr, *, out_scale):
    i = pl.program_id(2)
    d = DIFF_HEAD_DIM
    tq, tk = ATT_TQ, ATT_TK
    q = q_ref[...]
    qs = [q[:, c * d:(c + 1) * d] for c in range(2)]
    m_scr[...] = jnp.full(m_scr.shape, -jnp.inf, F32)
    l_scr[...] = jnp.zeros(l_scr.shape, F32)
    acc_scr[...] = jnp.zeros(acc_scr.shape, F32)

    def tile(j, masked):
        k = k_ref[pl.ds(pl.multiple_of(j * tk, tk), tk), :]
        vt = vt_ref[0, 0, j]
        for c in range(2):
            st = lax.dot_general(k[:, c * d:(c + 1) * d], qs[c], (((1,), (1,)), ((), ())),
                                 preferred_element_type=F32)
            if masked:
                kpos = lax.broadcasted_iota(jnp.int32, (tk, tq), 0)
                qpos = lax.broadcasted_iota(jnp.int32, (tk, tq), 1)
                st = jnp.where(kpos <= qpos, st, -jnp.inf)
            m_prev = m_scr[c]
            m_new = jnp.maximum(m_prev, jnp.max(st, axis=0, keepdims=True))
            alpha = jnp.exp2(m_prev - m_new)
            p = jnp.exp2(st - m_new)
            l_scr[c] = alpha * l_scr[c] + jnp.sum(p, axis=0, keepdims=True)
            acc_scr[c] = alpha * acc_scr[c] + jnp.dot(vt, p.astype(BF16),
                                                      preferred_element_type=F32)
            m_scr[c] = m_new

    def body(j, carry):
        tile(j, False)
        return carry

    lax.fori_loop(0, i, body, 0)
    tile(i, True)

    lam = lam_ref[0]
    o = acc_scr[0] / l_scr[0] - lam * (acc_scr[1] / l_scr[1])
    o = o * lax.rsqrt(jnp.mean(o * o, axis=0, keepdims=True) + NORM_EPS) * (g_ref[...] * out_scale)
    o_ref[...] = o.T.astype(o_ref.dtype)


def diff_attn_core(q, kv, lam, subln_g, *, batch, seq, out_scale):
    t, width = q.shape
    hw = 2 * DIFF_HEAD_DIM
    assert ATT_TQ == ATT_TK
    nq, nk = seq // ATT_TQ, seq // ATT_TK
    vt = kv[:, width:].reshape(batch, nk, ATT_TK, DIFF_HEADS, hw).transpose(0, 3, 1, 4, 2)
    return pl.pallas_call(
        functools.partial(_attn_kernel, out_scale=out_scale),
        grid=(batch, DIFF_HEADS, nq),
        in_specs=[pl.BlockSpec(memory_space=pltpu.SMEM),
                  pl.BlockSpec((ATT_TQ, hw), lambda b, h, i: (b * nq + i, h)),
                  pl.BlockSpec((seq, hw), lambda b, h, i: (b, h)),
                  pl.BlockSpec((1, 1, nk, hw, ATT_TK), lambda b, h, i: (b, h, 0, 0, 0)),
                  pl.BlockSpec((hw, 1), lambda b, h, i: (0, 0))],
        out_specs=pl.BlockSpec((ATT_TQ, hw), lambda b, h, i: (b * nq + i, h)),
        out_shape=jax.ShapeDtypeStruct((t, width), BF16),
        scratch_shapes=[pltpu.VMEM((2, 1, ATT_TQ), F32), pltpu.VMEM((2, 1, ATT_TQ), F32),
                        pltpu.VMEM((2, hw, ATT_TQ), F32)],
        compiler_params=_cparams(("arbitrary", "arbitrary", "arbitrary")),
        name="diff_attn",
    )(lam, q, kv, vt, subln_g.reshape(hw, 1))


PEER_TT = 512
PEER_TE = 1024
_CAND_COUNTS = tuple(min(PEER_TOPK, (PEER_TOPK + 1) // (i + 1)) for i in range(PEER_TOPK))
_N_CAND = sum(_CAND_COUNTS)
_N_CAND_PAD = -(-_N_CAND // SUBLANES) * SUBLANES


def _extract_top(x, n):
    rows = x.shape[0]
    iota = lax.broadcasted_iota(jnp.int32, x.shape, 0)
    vals = []
    for _ in range(n):
        m = jnp.max(x, axis=0, keepdims=True)
        idx = jnp.min(jnp.where(x == m, iota, rows), axis=0, keepdims=True)
        x = jnp.where(iota == idx, -jnp.inf, x)
        vals.append(m)
    return vals


def _peer_topk_kernel(q_ref, keys_ref, s1_ref, s2_ref, tau_ref):
    tt = q_ref.shape[0]
    kparts = [_split3(keys_ref[j])[:2] for j in range(2)]
    for p in range(PEER_HEADS):
        sc = []
        for j in range(2):
            c0 = (2 * p + j) * PEER_HALF
            qh, ql = _split3(q_ref[:, c0:c0 + PEER_HALF])[:2]
            kh, kl = kparts[j]
            nt = lambda a, b: lax.dot_general(a, b, (((1,), (1,)), ((), ())),
                                              preferred_element_type=F32)
            sc.append(nt(kh, qh) + (nt(kh, ql) + nt(kl, qh)))
        v1 = _extract_top(sc[0], PEER_TOPK)
        v2 = jnp.concatenate(_extract_top(sc[1], PEER_TOPK), axis=0)
        cand = [v1[i] + v2[0:_CAND_COUNTS[i]] for i in range(PEER_TOPK)]
        if _N_CAND_PAD > _N_CAND:
            cand.append(jnp.full((_N_CAND_PAD - _N_CAND, tt), -jnp.inf, F32))
        best = _extract_top(jnp.concatenate(cand, axis=0), PEER_TOPK + 1)
        zsum = sum(jnp.exp(bk - best[0]) for bk in best[:PEER_TOPK])
        mz = best[0] + jnp.log(zsum)
        tau = 0.5 * (best[PEER_TOPK - 1] + best[PEER_TOPK])
        s1_ref[0, p] = (sc[0] - mz) * LOG2E
        s2_ref[0, p] = sc[1] * LOG2E
        tau_ref[0, p:p + 1, :] = (tau - mz) * LOG2E


def peer_topk(q, sub_keys):
    t = q.shape[0]
    tt = PEER_TT
    nt = t // tt
    big = jax.ShapeDtypeStruct((nt, PEER_HEADS, N_KEYS, tt), F32)
    return pl.pallas_call(
        _peer_topk_kernel,
        grid=(nt,),
        in_specs=[pl.BlockSpec((tt, q.shape[1]), lambda i: (i, 0)),
                  pl.BlockSpec((2, N_KEYS, PEER_HALF), lambda i: (0, 0, 0))],
        out_specs=[pl.BlockSpec((1, PEER_HEADS, N_KEYS, tt), lambda i: (i, 0, 0, 0)),
                   pl.BlockSpec((1, PEER_HEADS, N_KEYS, tt), lambda i: (i, 0, 0, 0)),
                   pl.BlockSpec((1, PEER_HEADS, tt), lambda i: (i, 0, 0))],
        out_shape=[big, big, jax.ShapeDtypeStruct((nt, PEER_HEADS, tt), F32)],
        compiler_params=_cparams(("arbitrary",)),
        name="peer_topk",
    )(q, sub_keys)


def _peer_expert_kernel(h_ref, s1_ref, s2_ref, tau_ref, u_ref, vt_ref, x_ref, gate_ref, o_ref, acc,
                        act_scr, w_scr):
    e = pl.program_id(1)
    ne = pl.num_programs(1)
    tt = h_ref.shape[0]
    a_per = PEER_TE // N_KEYS
    gb = 32

    @pl.when(e == 0)
    def _():
        acc[...] = jnp.zeros_like(acc)

    act_scr[...] = lax.dot_general(u_ref[...], h_ref[...], (((1,), (1,)), ((), ())),
                                   preferred_element_type=F32)
    for ai in range(a_per):
        a = e * a_per + ai
        for tc in range(tt // LANES):
            cols = slice(tc * LANES, (tc + 1) * LANES)
            for bi in range(N_KEYS // gb):
                bs = slice(bi * gb, (bi + 1) * gb)
                rs = slice(ai * N_KEYS + bi * gb, ai * N_KEYS + (bi + 1) * gb)
                gsum = jnp.zeros((gb, LANES), F32)
                for p in range(PEER_HEADS):
                    s1_row = s1_ref[0, p, pl.ds(a, 1), :][:, cols]
                    tsum = s1_row + s2_ref[0, p, bs, cols]
                    gsum = gsum + jnp.where(tsum > tau_ref[0, p:p + 1, cols], jnp.exp2(tsum), 0.0)
                act = act_scr[rs, cols]
                gel = 0.5 * act * (1.0 + lax.erf(act * (2.0 ** -0.5)))
                w_scr[rs, cols] = (gsum * gel).astype(BF16)
    acc[...] += jnp.dot(vt_ref[...], w_scr[...], preferred_element_type=F32)

    @pl.when(e == ne - 1)
    def _():
        o_ref[...] = x_ref[...] + gate_ref[0] * acc[...].T


def peer_experts(h, s1, s2, tau, u_bf, vt_bf, x, gate, *, seq):
    t, d = h.shape
    n_exp = u_bf.shape[0]
    tt, te = PEER_TT, PEER_TE
    bidx = lambda i: (i * tt) // seq
    once = pl.Buffered(1)
    return pl.pallas_call(
        _peer_expert_kernel,
        grid=(t // tt, n_exp // te),
        in_specs=[pl.BlockSpec((tt, d), lambda i, e: (i, 0), pipeline_mode=once),
                  pl.BlockSpec((1, PEER_HEADS, N_KEYS, tt), lambda i, e: (i, 0, 0, 0),
                               pipeline_mode=once),
                  pl.BlockSpec((1, PEER_HEADS, N_KEYS, tt), lambda i, e: (i, 0, 0, 0),
                               pipeline_mode=once),
                  pl.BlockSpec((1, PEER_HEADS, tt), lambda i, e: (i, 0, 0)),
                  pl.BlockSpec((te, d), lambda i, e: (e, 0)),
                  pl.BlockSpec((d, te), lambda i, e: (0, e)),
                  pl.BlockSpec((tt, d), lambda i, e: (i, 0), pipeline_mode=once),
                  pl.BlockSpec((1, 1, d), lambda i, e: (bidx(i), 0, 0))],
        out_specs=pl.BlockSpec((tt, d), lambda i, e: (i, 0)),
        out_shape=jax.ShapeDtypeStruct((t, d), F32),
        scratch_shapes=[pltpu.VMEM((d, tt), F32),
                        pltpu.VMEM((te, tt), F32), pltpu.VMEM((te, tt), BF16)],
        compiler_params=_cparams(("arbitrary", "arbitrary")),
        name="peer_experts",
    )(h, s1, s2, tau, u_bf, vt_bf, x, gate)


def _rmsnorm_kernel(x_ref, g_ref, o_ref):
    x = x_ref[...]
    o_ref[...] = x * lax.rsqrt(jnp.mean(x * x, axis=-1, keepdims=True) + NORM_EPS) * g_ref[...]


def final_rmsnorm(x, g, tm=512):
    t, d = x.shape
    return pl.pallas_call(
        _rmsnorm_kernel,
        grid=(t // tm,),
        in_specs=[pl.BlockSpec((tm, d), lambda i: (i, 0)), pl.BlockSpec((1, d), lambda i: (0, 0))],
        out_specs=pl.BlockSpec((tm, d), lambda i: (i, 0)),
        out_shape=jax.ShapeDtypeStruct((t, d), F32),
        compiler_params=_cparams(("arbitrary",)),
        name="final_rmsnorm",
    )(x, g.reshape(1, d))


def _rope_tables(positions):
    d = DIFF_HEAD_DIM
    inv_freq = ROPE_THETA ** (-jnp.arange(0, d, 2, dtype=F32) / d)
    ang = positions.astype(F32)[..., None] * inv_freq
    cos, sin = jnp.cos(ang), jnp.sin(ang)
    cosf = jnp.concatenate([cos, cos], axis=-1).reshape(-1, d)
    sins = jnp.concatenate([-sin, sin], axis=-1).reshape(-1, d)
    return cosf, sins


def kernel(x, c, positions, ada_w, ada_b, norm_mix_g, norm_ffn_g, gdn_w_in, gdn_conv_w, gdn_a_log,
           gdn_dt_bias, gdn_o_gain, gdn_w_out, kv_norm_g, kv_ada_w, kv_ada_b, kv_w, diff_w_q,
           diff_lambda, diff_subln_g, diff_w_out, peer_w_q, peer_sub_keys, peer_u, peer_v, final_g):
    batch, seq, d = x.shape
    depth = ada_w.shape[0]
    t = batch * seq
    xs = x.reshape(t, d)
    cosf, sins = _rope_tables(positions)

    c_pad = jnp.pad(c, ((0, SUBLANES - batch), (0, 0)))
    mods = adaln_all(c_pad, ada_w, ada_b)[:, :batch]
    kv_mods = adaln_all(c_pad, kv_ada_w[None], kv_ada_b[None])[0, :batch]
    mod = lambda m: m[:, None, :]

    kv_sh = None
    for l in range(depth):
        sh1, sc1, gt1, sh2, sc2, gt2 = [mod(m) for m in jnp.split(mods[l], 6, axis=-1)]
        norm1 = (norm_mix_g[l].reshape(1, d), sh1, sc1)
        if l < N_A_LAYERS:
            w_in = gdn_w_in[l]
            w4 = 4 * GDN_W
            proj = fused_matmul(xs, w_in[:, :w4].astype(BF16), seq=seq, norm=norm1, emit_h=True)
            proj, h1 = proj
            w_ab = jnp.pad(w_in[:, w4:], ((0, 0), (0, LANES - 2 * GDN_HEADS))).astype(BF16)
            ab = fused_matmul(h1, w_ab, seq=seq)
            o = gdn_core(proj, ab[:, :GDN_HEADS],
                         ab[:, GDN_HEADS:2 * GDN_HEADS], gdn_conv_w[l], gdn_a_log[l],
                         gdn_dt_bias[l], gdn_o_gain[l], batch=batch, seq=seq)
            xs = fused_matmul(o, gdn_w_out[l].astype(BF16), seq=seq, res=(xs, gt1))
        else:
            j = l - N_A_LAYERS
            lambda_init = 0.8 - 0.6 * math.exp(-0.3 * l)
            lp = diff_lambda[j].astype(F32)
            lam = jnp.exp(jnp.sum(lp[0] * lp[1])) - jnp.exp(jnp.sum(lp[2] * lp[3])) + lambda_init
            qw = DIFF_HEADS * 2 * DIFF_HEAD_DIM
            q = fused_matmul(xs, diff_w_q[j].astype(BF16), seq=seq, norm=norm1,
                             rope=(cosf, sins, qw, DIFF_HEAD_DIM ** -0.5 * LOG2E), out_dtype=BF16)
            o = diff_attn_core(q, kv_sh, lam.reshape(1), diff_subln_g[j], batch=batch,
                               seq=seq, out_scale=1.0 - lambda_init)
            xs = fused_matmul(o, diff_w_out[j].astype(BF16), seq=seq, res=(xs, gt1))

        norm2 = (norm_ffn_g[l].reshape(1, d), sh2, sc2)
        pq, h2 = fused_matmul(xs, peer_w_q[l].astype(BF16), seq=seq, norm=norm2, emit_h=True)
        s1, s2, tau = peer_topk(pq, peer_sub_keys[l])
        xs = peer_experts(h2, s1, s2, tau, peer_u[l].astype(BF16), peer_v[l].T.astype(BF16),
                          xs, gt2, seq=seq)

        if l == N_A_LAYERS - 1:
            kvsh, kvsc = [mod(m) for m in jnp.split(kv_mods, 2, axis=-1)]
            qw = DIFF_HEADS * 2 * DIFF_HEAD_DIM
            kv_sh = fused_matmul(xs, kv_w.astype(BF16), seq=seq,
                                 norm=(kv_norm_g.reshape(1, d), kvsh, kvsc),
                                 rope=(cosf, sins, qw, 1.0), out_dtype=BF16)

    return final_rmsnorm(xs, final_g).reshape(batch, seq, d)
```

```python
import functools
import math

import jax
import jax.numpy as jnp
from jax import lax
from jax.experimental import pallas as pl
from jax.experimental.pallas import tpu as pltpu

F32 = jnp.float32
BF16 = jnp.bfloat16

NORM_EPS = 1e-6
N_A_LAYERS = 2
GDN_HEADS = 16
GDN_HEAD_DIM = 128
CONV_K = 4
CHUNK = 64
DIFF_HEADS = 8
DIFF_HEAD_DIM = 128
ROPE_THETA = 10000.0
PEER_HEADS = 8
PEER_HALF = 128
N_KEYS = 128
PEER_TOPK = 16

LOG2E = math.log2(math.e)

LANES = 128
SUBLANES = 8
VMEM_LIMIT = 56 * 1024 * 1024


def _cparams(sem):
    return pltpu.CompilerParams(dimension_semantics=sem, vmem_limit_bytes=VMEM_LIMIT)


def _dot(a, b):
    return jnp.dot(a.astype(BF16), b.astype(BF16), preferred_element_type=F32)


def _dot_nt(a, b):
    return lax.dot_general(a.astype(BF16), b.astype(BF16), (((1,), (1,)), ((), ())),
                           preferred_element_type=F32)


def _dot_tn(a, b):
    return lax.dot_general(a.astype(BF16), b.astype(BF16), (((0,), (0,)), ((), ())),
                           preferred_element_type=F32)


def _split3(x):
    hi = x.astype(BF16)
    r1 = x - hi.astype(F32)
    mid = r1.astype(BF16)
    lo = (r1 - mid.astype(F32)).astype(BF16)
    return hi, mid, lo


def _sigmoid(x):
    return 1.0 / (1.0 + jnp.exp(-x))


def _softplus(x):
    return jnp.maximum(x, 0.0) + jnp.log(1.0 + jnp.exp(-jnp.abs(x)))


def _adaln_kernel(c_ref, w_ref, b_ref, o_ref):
    c = c_ref[...]
    s = c * _sigmoid(c)
    o_ref[0] = _dot(s, w_ref[0]) + b_ref[0]


def adaln_all(c_pad, w, b, tn=1024):
    nl, d, n = w.shape
    rows = c_pad.shape[0]
    return pl.pallas_call(
        _adaln_kernel,
        grid=(nl, n // tn),
        in_specs=[pl.BlockSpec((rows, d), lambda l, j: (0, 0)),
                  pl.BlockSpec((1, d, tn), lambda l, j: (l, 0, j)),
                  pl.BlockSpec((1, 1, tn), lambda l, j: (l, 0, j))],
        out_specs=pl.BlockSpec((1, rows, tn), lambda l, j: (l, 0, j)),
        out_shape=jax.ShapeDtypeStruct((nl, rows, n), F32),
        compiler_params=_cparams(("arbitrary", "arbitrary")),
        name="adaln",
    )(c_pad, w, b.reshape(nl, 1, n))


def _rope_tile(acc, cosf, sins, qscale):
    pieces = []
    for g in range(acc.shape[1] // LANES):
        xg = acc[:, g * LANES:(g + 1) * LANES]
        rot = pltpu.roll(xg, LANES // 2, axis=1)
        pieces.append((xg * cosf + rot * sins) * qscale)
    return jnp.concatenate(pieces, axis=1) if len(pieces) > 1 else pieces[0]


def _mm_kernel(*refs, has_norm, mode, emit_h, rope_tiles, qscale):
    it = iter(refs)
    x_ref = next(it)
    if has_norm:
        g_ref, sh_ref, sc_ref = next(it), next(it), next(it)
    w_ref = next(it)
    if mode == "rope":
        cos_ref, sin_ref = next(it), next(it)
    if mode == "res":
        res_ref, gate_ref = next(it), next(it)
    o_ref = next(it)
    if emit_h:
        hout_ref = next(it)
    if has_norm:
        h_scr = next(it)
    j = pl.program_id(1)

    if has_norm:
        @pl.when(j == 0)
        def _():
            x = x_ref[...]
            ms = jnp.mean(x * x, axis=-1, keepdims=True)
            y = x * lax.rsqrt(ms + NORM_EPS) * g_ref[...]
            h = (y * (1.0 + sc_ref[0]) + sh_ref[0]).astype(BF16)
            h_scr[...] = h
            if emit_h:
                hout_ref[...] = h
        lhs = h_scr[...]
    else:
        lhs = x_ref[...]

    acc = jnp.dot(lhs, w_ref[...], preferred_element_type=F32)
    if mode == "rope":
        @pl.when(j < rope_tiles)
        def _():
            o_ref[...] = _rope_tile(acc, cos_ref[...], sin_ref[...], qscale).astype(o_ref.dtype)

        @pl.when(j >= rope_tiles)
        def _():
            o_ref[...] = acc.astype(o_ref.dtype)
    elif mode == "res":
        o_ref[...] = (res_ref[...] + gate_ref[0] * acc).astype(o_ref.dtype)
    else:
        o_ref[...] = acc.astype(o_ref.dtype)


def fused_matmul(x, w, *, seq, norm=None, rope=None, res=None, out_dtype=F32, emit_h=False,
                 tm=1024, tn=512):
    t, k = x.shape
    n = w.shape[1]
    tn = min(tn, n)
    assert t % tm == 0 and n % tn == 0 and seq % tm == 0
    bidx = lambda i: (i * tm) // seq
    in_specs = [pl.BlockSpec((tm, k), lambda i, j: (i, 0))]
    args = [x]
    if norm is not None:
        g, sh, sc = norm
        in_specs += [pl.BlockSpec((1, k), lambda i, j: (0, 0)),
                     pl.BlockSpec((1, 1, k), lambda i, j: (bidx(i), 0, 0)),
                     pl.BlockSpec((1, 1, k), lambda i, j: (bidx(i), 0, 0))]
        args += [g, sh, sc]
    in_specs.append(pl.BlockSpec((k, tn), lambda i, j: (0, j)))
    args.append(w)
    mode, rope_tiles, qscale = "none", 0, 1.0
    if rope is not None:
        cosf, sins, rope_cols, qscale = rope
        mode, rope_tiles = "rope", rope_cols // tn
        in_specs += [pl.BlockSpec((tm, LANES), lambda i, j: (i, 0)),
                     pl.BlockSpec((tm, LANES), lambda i, j: (i, 0))]
        args += [cosf, sins]
    if res is not None:
        r, gate = res
        mode = "res"
        in_specs += [pl.BlockSpec((tm, tn), lambda i, j: (i, j)),
                     pl.BlockSpec((1, 1, tn), lambda i, j: (bidx(i), 0, j))]
        args += [r, gate]
    out_specs = [pl.BlockSpec((tm, tn), lambda i, j: (i, j))]
    out_shape = [jax.ShapeDtypeStruct((t, n), out_dtype)]
    if emit_h:
        out_specs.append(pl.BlockSpec((tm, k), lambda i, j: (i, 0)))
        out_shape.append(jax.ShapeDtypeStruct((t, k), BF16))
    scratch = [pltpu.VMEM((tm, k), BF16)] if norm is not None else []
    outs = pl.pallas_call(
        functools.partial(_mm_kernel, has_norm=norm is not None, mode=mode, emit_h=emit_h,
                          rope_tiles=rope_tiles, qscale=qscale),
        grid=(t // tm, n // tn),
        in_specs=in_specs, out_specs=out_specs, out_shape=out_shape,
        scratch_shapes=scratch,
        compiler_params=_cparams(("arbitrary", "arbitrary")),
        name="fused_mm_" + mode,
    )(*args)
    return outs if emit_h else outs[0]


GDN_ROWS = 256
GDN_W = GDN_HEADS * GDN_HEAD_DIM


def _gdn_kernel(qkv_ref, z_ref, a_ref, b_ref, at_ref, convw_ref, alog_ref, dtb_ref, alogc_ref,
                dtbc_ref, ogain_ref, o_ref, xbuf, qkvs, qn_s, kn_s, kb_s, qg_s, kd_s, kbgv_s, dec_s,
                l_s, a_s, tinv_s, tinvb_s, t1_s, w_s, u_s, unew_s, state, stateb):
    r = pl.program_id(1)
    rows = GDN_ROWS
    hd = GDN_HEAD_DIM
    nh = GDN_HEADS
    ncs = rows // CHUNK
    units = [(ci, h) for ci in range(ncs) for h in range(nh)]
    rsl = lambda ci: slice(ci * CHUNK, (ci + 1) * CHUNK)
    csl = lambda h: slice(h * hd, (h + 1) * hd)

    @pl.when(r == 0)
    def _():
        state[...] = jnp.zeros_like(state)
        stateb[...] = jnp.zeros_like(stateb)
        xbuf[0:SUBLANES, :] = jnp.zeros((SUBLANES, 3 * GDN_W), F32)

    cb = 512
    for c in range(3 * GDN_W // cb):
        cs = slice(c * cb, (c + 1) * cb)
        xbuf[SUBLANES:SUBLANES + rows, cs] = qkv_ref[:, cs].astype(F32)
        acc = convw_ref[3:4, cs] * xbuf[SUBLANES:SUBLANES + rows, cs]
        for i in range(CONV_K - 1):
            off = SUBLANES - (CONV_K - 1) + i
            acc = acc + convw_ref[i:i + 1, cs] * xbuf[off:off + rows, cs]
        qkvs[:, cs] = acc * _sigmoid(acc)
        xbuf[0:SUBLANES, cs] = xbuf[rows:rows + SUBLANES, cs]

    neg_a = -jnp.exp(alog_ref[...])
    g_all = neg_a * _softplus(a_ref[...] + dtb_ref[...])
    beta_all = _sigmoid(b_ref[...])
    neg_ac = -jnp.exp(alogc_ref[...])

    ii = lax.broadcasted_iota(jnp.int32, (CHUNK, CHUNK), 0)
    jj = lax.broadcasted_iota(jnp.int32, (CHUNK, CHUNK), 1)
    strict = ii > jj
    eye = ii == jj
    tril = jnp.where(ii >= jj, 1.0, 0.0).astype(BF16)
    triu = jnp.where(ii <= jj, 1.0, 0.0).astype(BF16)
    eye_f = jnp.where(eye, 1.0, 0.0).astype(F32)
    sib_masks = []
    for m in range(CHUNK.bit_length() - 1):
        sib_masks.append(((ii >> (m + 1)) == (jj >> (m + 1))) & ((ii >> m) != (jj >> m)) & strict)
    ogain = ogain_ref[...]

    glast = []
    for ci in range(ncs):
        rs = rsl(ci)
        g = g_all[rs]
        beta = beta_all[rs]
        gt = neg_ac * _softplus(at_ref[0, ci] + dtbc_ref[...])
        G = sum(jnp.dot(tril, p, preferred_element_type=F32) for p in _split3(g))
        GT = sum(jnp.dot(p, triu, preferred_element_type=F32) for p in _split3(gt))
        eG = jnp.exp(G)
        eGl = jnp.exp(G[CHUNK - 1:CHUNK, :] - G)
        glast.append(jnp.exp(G[CHUNK - 1:CHUNK, :]))
        for h in range(nh):
            cs = csl(h)
            q = qkvs[rs, cs]
            k = qkvs[rs, GDN_W + h * hd:GDN_W + (h + 1) * hd]
            v = qkvs[rs, 2 * GDN_W + h * hd:2 * GDN_W + (h + 1) * hd]
            q = q * lax.rsqrt(jnp.sum(q * q, axis=-1, keepdims=True) + NORM_EPS) * (hd ** -0.5)
            k = k * lax.rsqrt(jnp.sum(k * k, axis=-1, keepdims=True) + NORM_EPS)
            beta_c = beta[:, h:h + 1]
            eG_c = eG[:, h:h + 1]
            kb = k * beta_c
            qn_s[rs, cs] = q.astype(BF16)
            kn_s[rs, cs] = k.astype(BF16)
            kb_s[rs, cs] = kb.astype(BF16)
            qg_s[rs, cs] = (q * eG_c).astype(BF16)
            kd_s[rs, cs] = (k * eGl[:, h:h + 1]).astype(BF16)
            kbgv_s[rs, 2 * h * hd:(2 * h + 1) * hd] = (kb * eG_c).astype(BF16)
            kbgv_s[rs, (2 * h + 1) * hd:(2 * h + 2) * hd] = (v * beta_c).astype(BF16)
            diff = G[:, h:h + 1] - GT[h:h + 1, :]
            dec_s[ci * nh + h] = jnp.exp(jnp.where(strict, diff, -jnp.inf))

    nt = lambda a, b: lax.dot_general(a, b, (((1,), (1,)), ((), ())), preferred_element_type=F32)
    mm = lambda a, b: jnp.dot(a, b, preferred_element_type=F32)

    for u, (ci, h) in enumerate(units):
        rs, cs = rsl(ci), csl(h)
        dec = dec_s[u]
        kn = kn_s[rs, cs]
        lmat = nt(kb_s[rs, cs], kn) * dec
        l_s[u] = lmat
        t0 = eye_f - jnp.where(sib_masks[0], lmat, 0.0)
        tinv_s[u] = t0
        tinvb_s[u] = t0.astype(BF16)
        a_s[u] = (nt(qn_s[rs, cs], kn) * jnp.where(eye, 1.0, dec)).astype(BF16)

    for sm in sib_masks[1:]:
        for u in range(len(units)):
            t1_s[u] = mm(jnp.where(sm, l_s[u], 0.0).astype(BF16), tinvb_s[u]).astype(BF16)
        for u in range(len(units)):
            tn = tinv_s[u] - mm(tinvb_s[u], t1_s[u])
            tinv_s[u] = tn
            tinvb_s[u] = tn.astype(BF16)

    for u, (ci, h) in enumerate(units):
        rs, cs = rsl(ci), csl(h)
        wu = mm(tinvb_s[u], kbgv_s[rs, 2 * h * hd:(2 * h + 2) * hd])
        w_s[rs, cs] = wu[:, :hd].astype(BF16)
        u_s[rs, cs] = wu[:, hd:]

    for ci in range(ncs):
        rs = rsl(ci)
        for h in range(nh):
            cs = csl(h)
            unew_s[h] = (u_s[rs, cs] - mm(w_s[rs, cs], stateb[h])).astype(BF16)
        for h in range(nh):
            cs = csl(h)
            un = unew_s[h]
            o = mm(qg_s[rs, cs], stateb[h]) + mm(a_s[ci * nh + h], un)
            snew = glast[ci][:, h:h + 1] * state[h] + lax.dot_general(
                kd_s[rs, cs], un, (((0,), (0,)), ((), ())), preferred_element_type=F32)
            state[h] = snew
            stateb[h] = snew.astype(BF16)
            o = o * lax.rsqrt(jnp.mean(o * o, axis=-1, keepdims=True) + NORM_EPS) * ogain
            zh = z_ref[rs, cs].astype(F32)
            o_ref[rs, cs] = (o * (zh * _sigmoid(zh))).astype(o_ref.dtype)


def gdn_core(proj, a, b, conv_w, a_log, dt_bias, o_gain, *, batch, seq):
    t = proj.shape[0]
    rows = GDN_ROWS
    nr = seq // rows
    cpr = rows // CHUNK
    h = GDN_HEADS
    a_t = a.reshape(batch, seq // CHUNK, CHUNK, h).transpose(0, 1, 3, 2)
    a_t = a_t.reshape(batch * nr, cpr, h, CHUNK)
    row_map = lambda bi, r: (bi * nr + r, 0)
    full = lambda bi, r: (0, 0)
    return pl.pallas_call(
        _gdn_kernel,
        grid=(batch, nr),
        in_specs=[pl.BlockSpec((rows, 3 * GDN_W), row_map),
                  pl.BlockSpec((rows, GDN_W), lambda bi, r: (bi * nr + r, 3)),
                  pl.BlockSpec((rows, h), row_map),
                  pl.BlockSpec((rows, h), row_map),
                  pl.BlockSpec((1, cpr, h, CHUNK), lambda bi, r: (bi * nr + r, 0, 0, 0)),
                  pl.BlockSpec((CONV_K, 3 * GDN_W), full),
                  pl.BlockSpec((1, h), full), pl.BlockSpec((1, h), full),
                  pl.BlockSpec((h, 1), full), pl.BlockSpec((h, 1), full),
                  pl.BlockSpec((1, GDN_HEAD_DIM), full)],
        out_specs=pl.BlockSpec((rows, GDN_W), row_map),
        out_shape=jax.ShapeDtypeStruct((t, GDN_W), BF16),
        scratch_shapes=[pltpu.VMEM((SUBLANES + rows, 3 * GDN_W), F32),
                        pltpu.VMEM((rows, 3 * GDN_W), F32),
                        pltpu.VMEM((rows, GDN_W), BF16),
                        pltpu.VMEM((rows, GDN_W), BF16),
                        pltpu.VMEM((rows, GDN_W), BF16),
                        pltpu.VMEM((rows, GDN_W), BF16),
                        pltpu.VMEM((rows, GDN_W), BF16),
                        pltpu.VMEM((rows, 2 * GDN_W), BF16),
                        pltpu.VMEM((cpr * h, CHUNK, CHUNK), F32),
                        pltpu.VMEM((cpr * h, CHUNK, CHUNK), F32),
                        pltpu.VMEM((cpr * h, CHUNK, CHUNK), BF16),
                        pltpu.VMEM((cpr * h, CHUNK, CHUNK), F32),
                        pltpu.VMEM((cpr * h, CHUNK, CHUNK), BF16),
                        pltpu.VMEM((cpr * h, CHUNK, CHUNK), BF16),
                        pltpu.VMEM((rows, GDN_W), BF16),
                        pltpu.VMEM((rows, GDN_W), F32),
                        pltpu.VMEM((h, CHUNK, GDN_HEAD_DIM), BF16),
                        pltpu.VMEM((h, GDN_HEAD_DIM, GDN_HEAD_DIM), F32),
                        pltpu.VMEM((h, GDN_HEAD_DIM, GDN_HEAD_DIM), BF16)],
        compiler_params=_cparams(("arbitrary", "arbitrary")),
        name="gdn_core",
    )(proj, proj, a, b, a_t, conv_w, a_log.reshape(1, h), dt_bias.reshape(1, h),
      a_log.reshape(h, 1), dt_bias.reshape(h, 1), o_gain.reshape(1, GDN_HEAD_DIM))


ATT_TQ = 1024
ATT_TK = 1024


def _attn_kernel(lam_ref, q_ref, k_ref, vt_ref, g_ref, o_ref, m_scr, l_scr, acc_scr, *, out_scale):
    i = pl.program_id(2)
    d = DIFF_HEAD_DIM
    tq, tk = ATT_TQ, ATT_TK
    q = q_ref[...]
    qs = [q[:, c * d:(c + 1) * d] for c in range(2)]
    m_scr[...] = jnp.full(m_scr.shape, -jnp.inf, F32)
    l_scr[...] = jnp.zeros(l_scr.shape, F32)
    acc_scr[...] = jnp.zeros(acc_scr.shape, F32)

    def tile(j, masked):
        k = k_ref[pl.ds(pl.multiple_of(j * tk, tk), tk), :]
        vt = vt_ref[0, 0, j]
        for c in range(2):
            st = lax.dot_general(k[:, c * d:(c + 1) * d], qs[c], (((1,), (1,)), ((), ())),
                                 preferred_element_type=F32)
            if masked:
                kpos = lax.broadcasted_iota(jnp.int32, (tk, tq), 0)
                qpos = lax.broadcasted_iota(jnp.int32, (tk, tq), 1)
                st = jnp.where(kpos <= qpos, st, -jnp.inf)
            m_prev = m_scr[c]
            m_new = jnp.maximum(m_prev, jnp.max(st, axis=0, keepdims=True))
            alpha = jnp.exp2(m_prev - m_new)
            p = jnp.exp2(st - m_new)
            l_scr[c] = alpha * l_scr[c] + jnp.sum(p, axis=0, keepdims=True)
            acc_scr[c] = alpha * acc_scr[c] + jnp.dot(vt, p.astype(BF16),
                                                      preferred_element_type=F32)
            m_scr[c] = m_new

    def body(j, carry):
        tile(j, False)
        return carry

    lax.fori_loop(0, i, body, 0)
    tile(i, True)

    lam = lam_ref[0]
    o = acc_scr[0] / l_scr[0] - lam * (acc_scr[1] / l_scr[1])
    o = o * lax.rsqrt(jnp.mean(o * o, axis=0, keepdims=True) + NORM_EPS) * (g_ref[...] * out_scale)
    o_ref[...] = o.T.astype(o_ref.dtype)


def diff_attn_core(q, kv, lam, subln_g, *, batch, seq, out_scale):
    t, width = q.shape
    hw = 2 * DIFF_HEAD_DIM
    assert ATT_TQ == ATT_TK
    nq, nk = seq // ATT_TQ, seq // ATT_TK
    vt = kv[:, width:].reshape(batch, nk, ATT_TK, DIFF_HEADS, hw).transpose(0, 3, 1, 4, 2)
    return pl.pallas_call(
        functools.partial(_attn_kernel, out_scale=out_scale),
        grid=(batch, DIFF_HEADS, nq),
        in_specs=[pl.BlockSpec(memory_space=pltpu.SMEM),
                  pl.BlockSpec((ATT_TQ, hw), lambda b, h, i: (b * nq + i, h)),
                  pl.BlockSpec((seq, hw), lambda b, h, i: (b, h)),
                  pl.BlockSpec((1, 1, nk, hw, ATT_TK), lambda b, h, i: (b, h, 0, 0, 0)),
                  pl.BlockSpec((hw, 1), lambda b, h, i: (0, 0))],
        out_specs=pl.BlockSpec((ATT_TQ, hw), lambda b, h, i: (b * nq + i, h)),
        out_shape=jax.ShapeDtypeStruct((t, width), BF16),
        scratch_shapes=[pltpu.VMEM((2, 1, ATT_TQ), F32), pltpu.VMEM((2, 1, ATT_TQ), F32),
                        pltpu.VMEM((2, hw, ATT_TQ), F32)],
        compiler_params=_cparams(("arbitrary", "arbitrary", "arbitrary")),
        name="diff_attn",
    )(lam, q, kv, vt, subln_g.reshape(hw, 1))


PEER_TT = 512
PEER_TE = 1024
_CAND_COUNTS = tuple(min(PEER_TOPK, (PEER_TOPK + 1) // (i + 1)) for i in range(PEER_TOPK))
_N_CAND = sum(_CAND_COUNTS)
_N_CAND_PAD = -(-_N_CAND // SUBLANES) * SUBLANES


def _extract_top(x, n):
    rows = x.shape[0]
    iota = lax.broadcasted_iota(jnp.int32, x.shape, 0)
    vals = []
    for _ in range(n):
        m = jnp.max(x, axis=0, keepdims=True)
        idx = jnp.min(jnp.where(x == m, iota, rows), axis=0, keepdims=True)
        x = jnp.where(iota == idx, -jnp.inf, x)
        vals.append(m)
    return vals


def _oddeven_merge_sort_pairs(n):
    pairs = []
    p = 1
    while p < n:
        k = p
        while k >= 1:
            for j in range(k % p, n - k, 2 * k):
                for i in range(min(k, n - j - k)):
                    if (i + j) // (2 * p) == (i + j + k) // (2 * p):
                        pairs.append((i + j, i + j + k))
            k //= 2
        p *= 2
    return pairs


_SORT16 = tuple(_oddeven_merge_sort_pairs(PEER_TOPK))


def _top16_sorted(x):
    n = PEER_TOPK
    vs = [x[SUBLANES * r:SUBLANES * (r + 1), :] for r in range(n)]
    for i, j in _SORT16:
        vs[i], vs[j] = jnp.maximum(vs[i], vs[j]), jnp.minimum(vs[i], vs[j])
    shift = SUBLANES // 2
    while shift >= 1:
        vs = [jnp.maximum(vs[i], pltpu.roll(vs[n - 1 - i], shift, axis=0)) for i in range(n)]
        k = n // 2
        while k >= 1:
            for i in range(n):
                if i & k == 0:
                    vs[i], vs[i + k] = jnp.maximum(vs[i], vs[i + k]), jnp.minimum(vs[i], vs[i + k])
            k //= 2
        shift //= 2
    return vs


def _peer_topk_kernel(q_ref, keys_ref, s1_ref, s2_ref, tau_ref):
    tt = q_ref.shape[0]
    kparts = [_split3(keys_ref[j])[:2] for j in range(2)]
    for p in range(PEER_HEADS):
        sc = []
        for j in range(2):
            c0 = (2 * p + j) * PEER_HALF
            qh, ql = _split3(q_ref[:, c0:c0 + PEER_HALF])[:2]
            kh, kl = kparts[j]
            nt = lambda a, b: lax.dot_general(a, b, (((1,), (1,)), ((), ())),
                                              preferred_element_type=F32)
            sc.append(nt(kh, qh) + (nt(kh, ql) + nt(kl, qh)))
        v1 = [v[0:1, :] for v in _top16_sorted(sc[0])]
        v2 = jnp.concatenate([v[0:1, :] for v in _top16_sorted(sc[1])], axis=0)
        cand = [v1[i] + v2[0:_CAND_COUNTS[i]] for i in range(PEER_TOPK)]
        if _N_CAND_PAD > _N_CAND:
            cand.append(jnp.full((_N_CAND_PAD - _N_CAND, tt), -jnp.inf, F32))
        best = _extract_top(jnp.concatenate(cand, axis=0), PEER_TOPK + 1)
        zsum = sum(jnp.exp(bk - best[0]) for bk in best[:PEER_TOPK])
        mz = best[0] + jnp.log(zsum)
        tau = 0.5 * (best[PEER_TOPK - 1] + best[PEER_TOPK])
        s1_ref[0, p] = (sc[0] - mz) * LOG2E
        s2_ref[0, p] = sc[1] * LOG2E
        tau_ref[0, p:p + 1, :] = (tau - mz) * LOG2E


def peer_topk(q, sub_keys):
    t = q.shape[0]
    tt = PEER_TT
    nt = t // tt
    big = jax.ShapeDtypeStruct((nt, PEER_HEADS, N_KEYS, tt), F32)
    return pl.pallas_call(
        _peer_topk_kernel,
        grid=(nt,),
        in_specs=[pl.BlockSpec((tt, q.shape[1]), lambda i: (i, 0)),
                  pl.BlockSpec((2, N_KEYS, PEER_HALF), lambda i: (0, 0, 0))],
        out_specs=[pl.BlockSpec((1, PEER_HEADS, N_KEYS, tt), lambda i: (i, 0, 0, 0)),
                   pl.BlockSpec((1, PEER_HEADS, N_KEYS, tt), lambda i: (i, 0, 0, 0)),
                   pl.BlockSpec((1, PEER_HEADS, tt), lambda i: (i, 0, 0))],
        out_shape=[big, big, jax.ShapeDtypeStruct((nt, PEER_HEADS, tt), F32)],
        compiler_params=_cparams(("arbitrary",)),
        name="peer_topk",
    )(q, sub_keys)


def _peer_expert_kernel(h_ref, s1_ref, s2_ref, tau_ref, u_ref, vt_ref, x_ref, gate_ref, o_ref, acc,
                        act_scr, w_scr):
    e = pl.program_id(1)
    ne = pl.num_programs(1)
    tt = h_ref.shape[0]
    a_per = PEER_TE // N_KEYS
    gb = 32

    @pl.when(e == 0)
    def _():
        acc[...] = jnp.zeros_like(acc)

    act_scr[...] = lax.dot_general(u_ref[...], h_ref[...], (((1,), (1,)), ((), ())),
                                   preferred_element_type=F32)
    for ai in range(a_per):
        a = e * a_per + ai
        for tc in range(tt // LANES):
            cols = slice(tc * LANES, (tc + 1) * LANES)
            for bi in range(N_KEYS // gb):
                bs = slice(bi * gb, (bi + 1) * gb)
                rs = slice(ai * N_KEYS + bi * gb, ai * N_KEYS + (bi + 1) * gb)
                gsum = jnp.zeros((gb, LANES), F32)
                for p in range(PEER_HEADS):
                    s1_row = s1_ref[0, p, pl.ds(a, 1), :][:, cols]
                    tsum = s1_row + s2_ref[0, p, bs, cols]
                    gsum = gsum + jnp.where(tsum > tau_ref[0, p:p + 1, cols], jnp.exp2(tsum), 0.0)
                act = act_scr[rs, cols]
                gel = 0.5 * act * (1.0 + lax.erf(act * (2.0 ** -0.5)))
                w_scr[rs, cols] = (gsum * gel).astype(BF16)
    acc[...] += jnp.dot(vt_ref[...], w_scr[...], preferred_element_type=F32)

    @pl.when(e == ne - 1)
    def _():
        o_ref[...] = x_ref[...] + gate_ref[0] * acc[...].T


def peer_experts(h, s1, s2, tau, u_bf, vt_bf, x, gate, *, seq):
    t, d = h.shape
    n_exp = u_bf.shape[0]
    tt, te = PEER_TT, PEER_TE
    bidx = lambda i: (i * tt) // seq
    once = pl.Buffered(1)
    return pl.pallas_call(
        _peer_expert_kernel,
        grid=(t // tt, n_exp // te),
        in_specs=[pl.BlockSpec((tt, d), lambda i, e: (i, 0), pipeline_mode=once),
                  pl.BlockSpec((1, PEER_HEADS, N_KEYS, tt), lambda i, e: (i, 0, 0, 0),
                               pipeline_mode=once),
                  pl.BlockSpec((1, PEER_HEADS, N_KEYS, tt), lambda i, e: (i, 0, 0, 0),
                               pipeline_mode=once),
                  pl.BlockSpec((1, PEER_HEADS, tt), lambda i, e: (i, 0, 0)),
                  pl.BlockSpec((te, d), lambda i, e: (e, 0)),
                  pl.BlockSpec((d, te), lambda i, e: (0, e)),
                  pl.BlockSpec((tt, d), lambda i, e: (i, 0), pipeline_mode=once),
                  pl.BlockSpec((1, 1, d), lambda i, e: (bidx(i), 0, 0))],
        out_specs=pl.BlockSpec((tt, d), lambda i, e: (i, 0)),
        out_shape=jax.ShapeDtypeStruct((t, d), F32),
        scratch_shapes=[pltpu.VMEM((d, tt), F32),
                        pltpu.VMEM((te, tt), F32), pltpu.VMEM((te, tt), BF16)],
        compiler_params=_cparams(("arbitrary", "arbitrary")),
        name="peer_experts",
    )(h, s1, s2, tau, u_bf, vt_bf, x, gate)


def _rmsnorm_kernel(x_ref, g_ref, o_ref):
    x = x_ref[...]
    o_ref[...] = x * lax.rsqrt(jnp.mean(x * x, axis=-1, keepdims=True) + NORM_EPS) * g_ref[...]


def final_rmsnorm(x, g, tm=512):
    t, d = x.shape
    return pl.pallas_call(
        _rmsnorm_kernel,
        grid=(t // tm,),
        in_specs=[pl.BlockSpec((tm, d), lambda i: (i, 0)), pl.BlockSpec((1, d), lambda i: (0, 0))],
        out_specs=pl.BlockSpec((tm, d), lambda i: (i, 0)),
        out_shape=jax.ShapeDtypeStruct((t, d), F32),
        compiler_params=_cparams(("arbitrary",)),
        name="final_rmsnorm",
    )(x, g.reshape(1, d))


def _rope_tables(positions):
    d = DIFF_HEAD_DIM
    inv_freq = ROPE_THETA ** (-jnp.arange(0, d, 2, dtype=F32) / d)
    ang = positions.astype(F32)[..., None] * inv_freq
    cos, sin = jnp.cos(ang), jnp.sin(ang)
    cosf = jnp.concatenate([cos, cos], axis=-1).reshape(-1, d)
    sins = jnp.concatenate([-sin, sin], axis=-1).reshape(-1, d)
    return cosf, sins


def kernel(x, c, positions, ada_w, ada_b, norm_mix_g, norm_ffn_g, gdn_w_in, gdn_conv_w, gdn_a_log,
           gdn_dt_bias, gdn_o_gain, gdn_w_out, kv_norm_g, kv_ada_w, kv_ada_b, kv_w, diff_w_q,
           diff_lambda, diff_subln_g, diff_w_out, peer_w_q, peer_sub_keys, peer_u, peer_v, final_g):
    batch, seq, d = x.shape
    depth = ada_w.shape[0]
    t = batch * seq
    xs = x.reshape(t, d)
    cosf, sins = _rope_tables(positions)

    c_pad = jnp.pad(c, ((0, SUBLANES - batch), (0, 0)))
    mods = adaln_all(c_pad, ada_w, ada_b)[:, :batch]
    kv_mods = adaln_all(c_pad, kv_ada_w[None], kv_ada_b[None])[0, :batch]
    mod = lambda m: m[:, None, :]

    kv_sh = None
    for l in range(depth):
        sh1, sc1, gt1, sh2, sc2, gt2 = [mod(m) for m in jnp.split(mods[l], 6, axis=-1)]
        norm1 = (norm_mix_g[l].reshape(1, d), sh1, sc1)
        if l < N_A_LAYERS:
            w_in = gdn_w_in[l]
            w4 = 4 * GDN_W
            proj = fused_matmul(xs, w_in[:, :w4].astype(BF16), seq=seq, norm=norm1, emit_h=True,
                                out_dtype=BF16)
            proj, h1 = proj
            w_ab = jnp.pad(w_in[:, w4:], ((0, 0), (0, LANES - 2 * GDN_HEADS))).astype(BF16)
            ab = fused_matmul(h1, w_ab, seq=seq)
            o = gdn_core(proj, ab[:, :GDN_HEADS],
                         ab[:, GDN_HEADS:2 * GDN_HEADS], gdn_conv_w[l], gdn_a_log[l],
                         gdn_dt_bias[l], gdn_o_gain[l], batch=batch, seq=seq)
            xs = fused_matmul(o, gdn_w_out[l].astype(BF16), seq=seq, res=(xs, gt1))
        else:
            j = l - N_A_LAYERS
            lambda_init = 0.8 - 0.6 * math.exp(-0.3 * l)
            lp = diff_lambda[j].astype(F32)
            lam = jnp.exp(jnp.sum(lp[0] * lp[1])) - jnp.exp(jnp.sum(lp[2] * lp[3])) + lambda_init
            qw = DIFF_HEADS * 2 * DIFF_HEAD_DIM
            q = fused_matmul(xs, diff_w_q[j].astype(BF16), seq=seq, norm=norm1,
                             rope=(cosf, sins, qw, DIFF_HEAD_DIM ** -0.5 * LOG2E), out_dtype=BF16)
            o = diff_attn_core(q, kv_sh, lam.reshape(1), diff_subln_g[j], batch=batch,
                               seq=seq, out_scale=1.0 - lambda_init)
            xs = fused_matmul(o, diff_w_out[j].astype(BF16), seq=seq, res=(xs, gt1))

        norm2 = (norm_ffn_g[l].reshape(1, d), sh2, sc2)
        pq, h2 = fused_matmul(xs, peer_w_q[l].astype(BF16), seq=seq, norm=norm2, emit_h=True)
        s1, s2, tau = peer_topk(pq, peer_sub_keys[l])
        xs = peer_experts(h2, s1, s2, tau, peer_u[l].astype(BF16), peer_v[l].T.astype(BF16),
                          xs, gt2, seq=seq)

        if l == N_A_LAYERS - 1:
            kvsh, kvsc = [mod(m) for m in jnp.split(kv_mods, 2, axis=-1)]
            qw = DIFF_HEADS * 2 * DIFF_HEAD_DIM
            kv_sh = fused_matmul(xs, kv_w.astype(BF16), seq=seq,
                                 norm=(kv_norm_g.reshape(1, d), kvsh, kvsc),
                                 rope=(cosf, sins, qw, 1.0), out_dtype=BF16)

    return final_rmsnorm(xs, final_g).reshape(batch, seq, d)
```

```python
import functools
import math

import jax
import jax.numpy as jnp
from jax import lax
from jax.experimental import pallas as pl
from jax.experimental.pallas import tpu as pltpu

F32 = jnp.float32
BF16 = jnp.bfloat16

NORM_EPS = 1e-6
N_A_LAYERS = 2
GDN_HEADS = 16
GDN_HEAD_DIM = 128
CONV_K = 4
CHUNK = 64
DIFF_HEADS = 8
DIFF_HEAD_DIM = 128
ROPE_THETA = 10000.0
PEER_HEADS = 8
PEER_HALF = 128
N_KEYS = 128
PEER_TOPK = 16

LOG2E = math.log2(math.e)

LANES = 128
SUBLANES = 8
VMEM_LIMIT = 56 * 1024 * 1024


def _cparams(sem):
    return pltpu.CompilerParams(dimension_semantics=sem, vmem_limit_bytes=VMEM_LIMIT)


def _dot(a, b):
    return jnp.dot(a.astype(BF16), b.astype(BF16), preferred_element_type=F32)


def _dot_nt(a, b):
    return lax.dot_general(a.astype(BF16), b.astype(BF16), (((1,), (1,)), ((), ())),
                           preferred_element_type=F32)


def _dot_tn(a, b):
    return lax.dot_general(a.astype(BF16), b.astype(BF16), (((0,), (0,)), ((), ())),
                           preferred_element_type=F32)


def _split3(x):
    hi = x.astype(BF16)
    r1 = x - hi.astype(F32)
    mid = r1.astype(BF16)
    lo = (r1 - mid.astype(F32)).astype(BF16)
    return hi, mid, lo


def _sigmoid(x):
    return 1.0 / (1.0 + jnp.exp(-x))


def _softplus(x):
    return jnp.maximum(x, 0.0) + jnp.log(1.0 + jnp.exp(-jnp.abs(x)))


def _adaln_kernel(c_ref, w_ref, b_ref, o_ref):
    c = c_ref[...]
    s = c * _sigmoid(c)
    o_ref[0] = _dot(s, w_ref[0]) + b_ref[0]


def adaln_all(c_pad, w, b, tn=1024):
    nl, d, n = w.shape
    rows = c_pad.shape[0]
    return pl.pallas_call(
        _adaln_kernel,
        grid=(nl, n // tn),
        in_specs=[pl.BlockSpec((rows, d), lambda l, j: (0, 0)),
                  pl.BlockSpec((1, d, tn), lambda l, j: (l, 0, j)),
                  pl.BlockSpec((1, 1, tn), lambda l, j: (l, 0, j))],
        out_specs=pl.BlockSpec((1, rows, tn), lambda l, j: (l, 0, j)),
        out_shape=jax.ShapeDtypeStruct((nl, rows, n), F32),
        compiler_params=_cparams(("arbitrary", "arbitrary")),
        name="adaln",
    )(c_pad, w, b.reshape(nl, 1, n))


def _rope_tile(acc, cosf, sins, qscale):
    pieces = []
    for g in range(acc.shape[1] // LANES):
        xg = acc[:, g * LANES:(g + 1) * LANES]
        rot = pltpu.roll(xg, LANES // 2, axis=1)
        pieces.append((xg * cosf + rot * sins) * qscale)
    return jnp.concatenate(pieces, axis=1) if len(pieces) > 1 else pieces[0]


def _mm_kernel(*refs, has_norm, mode, emit_h, rope_tiles, qscale, tail_hw):
    it = iter(refs)
    x_ref = next(it)
    if has_norm:
        g_ref, sh_ref, sc_ref = next(it), next(it), next(it)
    w_ref = next(it)
    if mode == "rope":
        cos_ref, sin_ref = next(it), next(it)
    if mode == "res":
        res_ref, gate_ref = next(it), next(it)
    o_ref = next(it)
    if emit_h:
        hout_ref = next(it)
    if tail_hw:
        tail_ref = next(it)
    if has_norm:
        h_scr = next(it)
    j = pl.program_id(1)

    if has_norm:
        @pl.when(j == 0)
        def _():
            x = x_ref[...]
            ms = jnp.mean(x * x, axis=-1, keepdims=True)
            y = x * lax.rsqrt(ms + NORM_EPS) * g_ref[...]
            h = (y * (1.0 + sc_ref[0]) + sh_ref[0]).astype(BF16)
            h_scr[...] = h
            if emit_h:
                hout_ref[...] = h
        lhs = h_scr[...]
    else:
        lhs = x_ref[...]

    acc = jnp.dot(lhs, w_ref[...], preferred_element_type=F32)
    if mode == "rope":
        @pl.when(j < rope_tiles)
        def _():
            o_ref[...] = _rope_tile(acc, cos_ref[...], sin_ref[...], qscale).astype(o_ref.dtype)

        if tail_hw:
            @pl.when(j >= rope_tiles)
            def _():
                tn, tm = acc.shape[1], acc.shape[0]
                tail_ref[0, :, 0] = acc.T.reshape(tn // tail_hw, tail_hw, tm).astype(tail_ref.dtype)
    elif mode == "res":
        o_ref[...] = (res_ref[...] + gate_ref[0] * acc).astype(o_ref.dtype)
    else:
        o_ref[...] = acc.astype(o_ref.dtype)


def fused_matmul(x, w, *, seq, norm=None, rope=None, res=None, out_dtype=F32, emit_h=False,
                 tail_hw=0, tm=1024, tn=512):
    t, k = x.shape
    n = w.shape[1]
    tn = min(tn, n)
    assert t % tm == 0 and n % tn == 0 and seq % tm == 0
    bidx = lambda i: (i * tm) // seq
    in_specs = [pl.BlockSpec((tm, k), lambda i, j: (i, 0))]
    args = [x]
    if norm is not None:
        g, sh, sc = norm
        in_specs += [pl.BlockSpec((1, k), lambda i, j: (0, 0)),
                     pl.BlockSpec((1, 1, k), lambda i, j: (bidx(i), 0, 0)),
                     pl.BlockSpec((1, 1, k), lambda i, j: (bidx(i), 0, 0))]
        args += [g, sh, sc]
    in_specs.append(pl.BlockSpec((k, tn), lambda i, j: (0, j)))
    args.append(w)
    mode, rope_tiles, qscale = "none", 0, 1.0
    if rope is not None:
        cosf, sins, rope_cols, qscale = rope
        mode, rope_tiles = "rope", rope_cols // tn
        in_specs += [pl.BlockSpec((tm, LANES), lambda i, j: (i, 0)),
                     pl.BlockSpec((tm, LANES), lambda i, j: (i, 0))]
        args += [cosf, sins]
    if res is not None:
        r, gate = res
        mode = "res"
        in_specs += [pl.BlockSpec((tm, tn), lambda i, j: (i, j)),
                     pl.BlockSpec((1, 1, tn), lambda i, j: (bidx(i), 0, j))]
        args += [r, gate]
    out_specs = [pl.BlockSpec((tm, tn), lambda i, j: (i, j))]
    out_shape = [jax.ShapeDtypeStruct((t, n), out_dtype)]
    if emit_h:
        out_specs.append(pl.BlockSpec((tm, k), lambda i, j: (i, 0)))
        out_shape.append(jax.ShapeDtypeStruct((t, k), BF16))
    if mode == "rope" and rope_cols < n:
        assert tail_hw and tn % tail_hw == 0 and (n - rope_cols) % tn == 0
        rpb = seq // tm
        out_specs[0] = pl.BlockSpec((tm, tn), lambda i, j: (i, jnp.minimum(j, rope_tiles - 1)))
        out_shape[0] = jax.ShapeDtypeStruct((t, rope_cols), out_dtype)
        out_specs.append(pl.BlockSpec(
            (1, tn // tail_hw, 1, tail_hw, tm),
            lambda i, j: (bidx(i), jnp.maximum(j - rope_tiles, 0), i - bidx(i) * rpb, 0, 0)))
        out_shape.append(jax.ShapeDtypeStruct(
            (t // seq, (n - rope_cols) // tail_hw, rpb, tail_hw, tm), out_dtype))
    else:
        tail_hw = 0
    scratch = [pltpu.VMEM((tm, k), BF16)] if norm is not None else []
    outs = pl.pallas_call(
        functools.partial(_mm_kernel, has_norm=norm is not None, mode=mode, emit_h=emit_h,
                          rope_tiles=rope_tiles, qscale=qscale, tail_hw=tail_hw),
        grid=(t // tm, n // tn),
        in_specs=in_specs, out_specs=out_specs, out_shape=out_shape,
        scratch_shapes=scratch,
        compiler_params=_cparams(("arbitrary", "arbitrary")),
        name="fused_mm_" + mode,
    )(*args)
    return outs if len(outs) > 1 else outs[0]


GDN_ROWS = 256
GDN_W = GDN_HEADS * GDN_HEAD_DIM


def _gdn_kernel(qkv_ref, z_ref, a_ref, b_ref, at_ref, convw_ref, alog_ref, dtb_ref, alogc_ref,
                dtbc_ref, ogain_ref, o_ref, xbuf, qkvs, qn_s, kn_s, kb_s, qg_s, kd_s, kbgv_s, dec_s,
                l_s, a_s, tinv_s, tinvb_s, t1_s, w_s, u_s, unew_s, state, stateb):
    r = pl.program_id(1)
    rows = GDN_ROWS
    hd = GDN_HEAD_DIM
    nh = GDN_HEADS
    ncs = rows // CHUNK
    units = [(ci, h) for ci in range(ncs) for h in range(nh)]
    rsl = lambda ci: slice(ci * CHUNK, (ci + 1) * CHUNK)
    csl = lambda h: slice(h * hd, (h + 1) * hd)

    @pl.when(r == 0)
    def _():
        state[...] = jnp.zeros_like(state)
        stateb[...] = jnp.zeros_like(stateb)
        xbuf[0:SUBLANES, :] = jnp.zeros((SUBLANES, 3 * GDN_W), F32)

    cb = 512
    for c in range(3 * GDN_W // cb):
        cs = slice(c * cb, (c + 1) * cb)
        xbuf[SUBLANES:SUBLANES + rows, cs] = qkv_ref[:, cs].astype(F32)
        acc = convw_ref[3:4, cs] * xbuf[SUBLANES:SUBLANES + rows, cs]
        for i in range(CONV_K - 1):
            off = SUBLANES - (CONV_K - 1) + i
            acc = acc + convw_ref[i:i + 1, cs] * xbuf[off:off + rows, cs]
        qkvs[:, cs] = acc * _sigmoid(acc)
        xbuf[0:SUBLANES, cs] = xbuf[rows:rows + SUBLANES, cs]

    neg_a = -jnp.exp(alog_ref[...])
    g_all = neg_a * _softplus(a_ref[...] + dtb_ref[...])
    beta_all = _sigmoid(b_ref[...])
    neg_ac = -jnp.exp(alogc_ref[...])

    ii = lax.broadcasted_iota(jnp.int32, (CHUNK, CHUNK), 0)
    jj = lax.broadcasted_iota(jnp.int32, (CHUNK, CHUNK), 1)
    strict = ii > jj
    eye = ii == jj
    tril = jnp.where(ii >= jj, 1.0, 0.0).astype(BF16)
    triu = jnp.where(ii <= jj, 1.0, 0.0).astype(BF16)
    eye_f = jnp.where(eye, 1.0, 0.0).astype(F32)
    sib_masks = []
    for m in range(CHUNK.bit_length() - 1):
        sib_masks.append(((ii >> (m + 1)) == (jj >> (m + 1))) & ((ii >> m) != (jj >> m)) & strict)
    ogain = ogain_ref[...]

    glast = []
    for ci in range(ncs):
        rs = rsl(ci)
        g = g_all[rs]
        beta = beta_all[rs]
        gt = neg_ac * _softplus(at_ref[0, ci] + dtbc_ref[...])
        G = sum(jnp.dot(tril, p, preferred_element_type=F32) for p in _split3(g))
        GT = sum(jnp.dot(p, triu, preferred_element_type=F32) for p in _split3(gt))
        eG = jnp.exp(G)
        eGl = jnp.exp(G[CHUNK - 1:CHUNK, :] - G)
        glast.append(jnp.exp(G[CHUNK - 1:CHUNK, :]))
        for h in range(nh):
            cs = csl(h)
            q = qkvs[rs, cs]
            k = qkvs[rs, GDN_W + h * hd:GDN_W + (h + 1) * hd]
            v = qkvs[rs, 2 * GDN_W + h * hd:2 * GDN_W + (h + 1) * hd]
            q = q * lax.rsqrt(jnp.sum(q * q, axis=-1, keepdims=True) + NORM_EPS) * (hd ** -0.5)
            k = k * lax.rsqrt(jnp.sum(k * k, axis=-1, keepdims=True) + NORM_EPS)
            beta_c = beta[:, h:h + 1]
            eG_c = eG[:, h:h + 1]
            kb = k * beta_c
            qn_s[rs, cs] = q.astype(BF16)
            kn_s[rs, cs] = k.astype(BF16)
            kb_s[rs, cs] = kb.astype(BF16)
            qg_s[rs, cs] = (q * eG_c).astype(BF16)
            kd_s[rs, cs] = (k * eGl[:, h:h + 1]).astype(BF16)
            kbgv_s[rs, 2 * h * hd:(2 * h + 1) * hd] = (kb * eG_c).astype(BF16)
            kbgv_s[rs, (2 * h + 1) * hd:(2 * h + 2) * hd] = (v * beta_c).astype(BF16)
            diff = G[:, h:h + 1] - GT[h:h + 1, :]
            dec_s[ci * nh + h] = jnp.exp(jnp.where(strict, diff, -jnp.inf))

    nt = lambda a, b: lax.dot_general(a, b, (((1,), (1,)), ((), ())), preferred_element_type=F32)
    mm = lambda a, b: jnp.dot(a, b, preferred_element_type=F32)

    for u, (ci, h) in enumerate(units):
        rs, cs = rsl(ci), csl(h)
        dec = dec_s[u]
        kn = kn_s[rs, cs]
        lmat = nt(kb_s[rs, cs], kn) * dec
        l_s[u] = lmat
        t0 = eye_f - jnp.where(sib_masks[0], lmat, 0.0)
        tinv_s[u] = t0
        tinvb_s[u] = t0.astype(BF16)
        a_s[u] = (nt(qn_s[rs, cs], kn) * jnp.where(eye, 1.0, dec)).astype(BF16)

    for sm in sib_masks[1:]:
        for u in range(len(units)):
            t1_s[u] = mm(jnp.where(sm, l_s[u], 0.0).astype(BF16), tinvb_s[u]).astype(BF16)
        for u in range(len(units)):
            tn = tinv_s[u] - mm(tinvb_s[u], t1_s[u])
            tinv_s[u] = tn
            tinvb_s[u] = tn.astype(BF16)

    for u, (ci, h) in enumerate(units):
        rs, cs = rsl(ci), csl(h)
        wu = mm(tinvb_s[u], kbgv_s[rs, 2 * h * hd:(2 * h + 2) * hd])
        w_s[rs, cs] = wu[:, :hd].astype(BF16)
        u_s[rs, cs] = wu[:, hd:]

    for ci in range(ncs):
        rs = rsl(ci)
        for h in range(nh):
            cs = csl(h)
            unew_s[h] = (u_s[rs, cs] - mm(w_s[rs, cs], stateb[h])).astype(BF16)
        for h in range(nh):
            cs = csl(h)
            un = unew_s[h]
            o = mm(qg_s[rs, cs], stateb[h]) + mm(a_s[ci * nh + h], un)
            snew = glast[ci][:, h:h + 1] * state[h] + lax.dot_general(
                kd_s[rs, cs], un, (((0,), (0,)), ((), ())), preferred_element_type=F32)
            state[h] = snew
            stateb[h] = snew.astype(BF16)
            o = o * lax.rsqrt(jnp.mean(o * o, axis=-1, keepdims=True) + NORM_EPS) * ogain
            zh = z_ref[rs, cs].astype(F32)
            o_ref[rs, cs] = (o * (zh * _sigmoid(zh))).astype(o_ref.dtype)


def gdn_core(proj, a, b, conv_w, a_log, dt_bias, o_gain, *, batch, seq):
    t = proj.shape[0]
    rows = GDN_ROWS
    nr = seq // rows
    cpr = rows // CHUNK
    h = GDN_HEADS
    a_t = a.reshape(batch, seq // CHUNK, CHUNK, h).transpose(0, 1, 3, 2)
    a_t = a_t.reshape(batch * nr, cpr, h, CHUNK)
    row_map = lambda bi, r: (bi * nr + r, 0)
    full = lambda bi, r: (0, 0)
    return pl.pallas_call(
        _gdn_kernel,
        grid=(batch, nr),
        in_specs=[pl.BlockSpec((rows, 3 * GDN_W), row_map),
                  pl.BlockSpec((rows, GDN_W), lambda bi, r: (bi * nr + r, 3)),
                  pl.BlockSpec((rows, h), row_map),
                  pl.BlockSpec((rows, h), row_map),
                  pl.BlockSpec((1, cpr, h, CHUNK), lambda bi, r: (bi * nr + r, 0, 0, 0)),
                  pl.BlockSpec((CONV_K, 3 * GDN_W), full),
                  pl.BlockSpec((1, h), full), pl.BlockSpec((1, h), full),
                  pl.BlockSpec((h, 1), full), pl.BlockSpec((h, 1), full),
                  pl.BlockSpec((1, GDN_HEAD_DIM), full)],
        out_specs=pl.BlockSpec((rows, GDN_W), row_map),
        out_shape=jax.ShapeDtypeStruct((t, GDN_W), BF16),
        scratch_shapes=[pltpu.VMEM((SUBLANES + rows, 3 * GDN_W), F32),
                        pltpu.VMEM((rows, 3 * GDN_W), F32),
                        pltpu.VMEM((rows, GDN_W), BF16),
                        pltpu.VMEM((rows, GDN_W), BF16),
                        pltpu.VMEM((rows, GDN_W), BF16),
                        pltpu.VMEM((rows, GDN_W), BF16),
                        pltpu.VMEM((rows, GDN_W), BF16),
                        pltpu.VMEM((rows, 2 * GDN_W), BF16),
                        pltpu.VMEM((cpr * h, CHUNK, CHUNK), F32),
                        pltpu.VMEM((cpr * h, CHUNK, CHUNK), F32),
                        pltpu.VMEM((cpr * h, CHUNK, CHUNK), BF16),
                        pltpu.VMEM((cpr * h, CHUNK, CHUNK), F32),
                        pltpu.VMEM((cpr * h, CHUNK, CHUNK), BF16),
                        pltpu.VMEM((cpr * h, CHUNK, CHUNK), BF16),
                        pltpu.VMEM((rows, GDN_W), BF16),
                        pltpu.VMEM((rows, GDN_W), F32),
                        pltpu.VMEM((h, CHUNK, GDN_HEAD_DIM), BF16),
                        pltpu.VMEM((h, GDN_HEAD_DIM, GDN_HEAD_DIM), F32),
                        pltpu.VMEM((h, GDN_HEAD_DIM, GDN_HEAD_DIM), BF16)],
        compiler_params=_cparams(("arbitrary", "arbitrary")),
        name="gdn_core",
    )(proj, proj, a, b, a_t, conv_w, a_log.reshape(1, h), dt_bias.reshape(1, h),
      a_log.reshape(h, 1), dt_bias.reshape(h, 1), o_gain.reshape(1, GDN_HEAD_DIM))


ATT_TQ = 1024
ATT_TK = 1024


def _attn_kernel(lam_ref, q_ref, k_ref, vt_ref, g_ref, o_ref, m_scr, l_scr, acc_scr, *, out_scale):
    i = pl.program_id(2)
    d = DIFF_HEAD_DIM
    tq, tk = ATT_TQ, ATT_TK
    q = q_ref[...]
    qs = [q[:, c * d:(c + 1) * d] for c in range(2)]
    m_scr[...] = jnp.full(m_scr.shape, -jnp.inf, F32)
    l_scr[...] = jnp.zeros(l_scr.shape, F32)
    acc_scr[...] = jnp.zeros(acc_scr.shape, F32)

    def tile(j, masked):
        k = k_ref[pl.ds(pl.multiple_of(j * tk, tk), tk), :]
        vt = vt_ref[0, 0, j]
        sts = [lax.dot_general(k[:, c * d:(c + 1) * d], qs[c], (((1,), (1,)), ((), ())),
                               preferred_element_type=F32) for c in range(2)]
        if masked:
            kpos = lax.broadcasted_iota(jnp.int32, (tk, tq), 0)
            qpos = lax.broadcasted_iota(jnp.int32, (tk, tq), 1)
            sts = [jnp.where(kpos <= qpos, st, -jnp.inf) for st in sts]
        ps, alphas = [], []
        for c in range(2):
            m_prev = m_scr[c]
            m_new = jnp.maximum(m_prev, jnp.max(sts[c], axis=0, keepdims=True))
            alpha = jnp.exp2(m_prev - m_new)
            p = jnp.exp2(sts[c] - m_new)
            l_scr[c] = alpha * l_scr[c] + jnp.sum(p, axis=0, keepdims=True)
            m_scr[c] = m_new
            ps.append(p.astype(BF16))
            alphas.append(alpha)
        for c in range(2):
            acc_scr[c] = alphas[c] * acc_scr[c] + jnp.dot(vt, ps[c],
                                                          preferred_element_type=F32)

    def body(j, carry):
        tile(j, False)
        return carry

    lax.fori_loop(0, i, body, 0)
    tile(i, True)

    lam = lam_ref[0]
    o = acc_scr[0] / l_scr[0] - lam * (acc_scr[1] / l_scr[1])
    o = o * lax.rsqrt(jnp.mean(o * o, axis=0, keepdims=True) + NORM_EPS) * (g_ref[...] * out_scale)
    o_ref[...] = o.T.astype(o_ref.dtype)


def diff_attn_core(q, k, vt, lam, subln_g, *, batch, seq, out_scale):
    t, width = q.shape
    hw = 2 * DIFF_HEAD_DIM
    assert ATT_TQ == ATT_TK
    nq, nk = seq // ATT_TQ, seq // ATT_TK
    assert vt.shape == (batch, DIFF_HEADS, nk, hw, ATT_TK)
    return pl.pallas_call(
        functools.partial(_attn_kernel, out_scale=out_scale),
        grid=(batch, DIFF_HEADS, nq),
        in_specs=[pl.BlockSpec(memory_space=pltpu.SMEM),
                  pl.BlockSpec((ATT_TQ, hw), lambda b, h, i: (b * nq + i, h)),
                  pl.BlockSpec((seq, hw), lambda b, h, i: (b, h)),
                  pl.BlockSpec((1, 1, nk, hw, ATT_TK), lambda b, h, i: (b, h, 0, 0, 0)),
                  pl.BlockSpec((hw, 1), lambda b, h, i: (0, 0))],
        out_specs=pl.BlockSpec((ATT_TQ, hw), lambda b, h, i: (b * nq + i, h)),
        out_shape=jax.ShapeDtypeStruct((t, width), BF16),
        scratch_shapes=[pltpu.VMEM((2, 1, ATT_TQ), F32), pltpu.VMEM((2, 1, ATT_TQ), F32),
                        pltpu.VMEM((2, hw, ATT_TQ), F32)],
        compiler_params=_cparams(("arbitrary", "arbitrary", "arbitrary")),
        name="diff_attn",
    )(lam, q, k, vt, subln_g.reshape(hw, 1))


PEER_TT = 512
PEER_TE = 1024
_CAND_COUNTS = tuple(min(PEER_TOPK, (PEER_TOPK + 1) // (i + 1)) for i in range(PEER_TOPK))
_N_CAND = sum(_CAND_COUNTS)
_N_CAND_PAD = -(-_N_CAND // SUBLANES) * SUBLANES


def _extract_top(x, n):
    rows = x.shape[0]
    iota = lax.broadcasted_iota(jnp.int32, x.shape, 0)
    vals = []
    for _ in range(n):
        m = jnp.max(x, axis=0, keepdims=True)
        idx = jnp.min(jnp.where(x == m, iota, rows), axis=0, keepdims=True)
        x = jnp.where(iota == idx, -jnp.inf, x)
        vals.append(m)
    return vals


def _oddeven_merge_sort_pairs(n):
    pairs = []
    p = 1
    while p < n:
        k = p
        while k >= 1:
            for j in range(k % p, n - k, 2 * k):
                for i in range(min(k, n - j - k)):
                    if (i + j) // (2 * p) == (i + j + k) // (2 * p):
                        pairs.append((i + j, i + j + k))
            k //= 2
        p *= 2
    return pairs


_SORT16 = tuple(_oddeven_merge_sort_pairs(PEER_TOPK))


def _top16_sorted(x):
    n = PEER_TOPK
    vs = [x[SUBLANES * r:SUBLANES * (r + 1), :] for r in range(n)]
    for i, j in _SORT16:
        vs[i], vs[j] = jnp.maximum(vs[i], vs[j]), jnp.minimum(vs[i], vs[j])
    shift = SUBLANES // 2
    while shift >= 1:
        vs = [jnp.maximum(vs[i], pltpu.roll(vs[n - 1 - i], shift, axis=0)) for i in range(n)]
        k = n // 2
        while k >= 1:
            for i in range(n):
                if i & k == 0:
                    vs[i], vs[i + k] = jnp.maximum(vs[i], vs[i + k]), jnp.minimum(vs[i], vs[i + k])
            k //= 2
        shift //= 2
    return vs


def _peer_topk_kernel(q_ref, keys_ref, s1_ref, s2_ref, tau_ref):
    tt = q_ref.shape[0]
    kparts = [_split3(keys_ref[j])[:2] for j in range(2)]
    for p in range(PEER_HEADS):
        sc = []
        for j in range(2):
            c0 = (2 * p + j) * PEER_HALF
            qh, ql = _split3(q_ref[:, c0:c0 + PEER_HALF])[:2]
            kh, kl = kparts[j]
            nt = lambda a, b: lax.dot_general(a, b, (((1,), (1,)), ((), ())),
                                              preferred_element_type=F32)
            sc.append(nt(kh, qh) + (nt(kh, ql) + nt(kl, qh)))
        v1 = [v[0:1, :] for v in _top16_sorted(sc[0])]
        v2 = jnp.concatenate([v[0:1, :] for v in _top16_sorted(sc[1])], axis=0)
        cand = [v1[i] + v2[0:_CAND_COUNTS[i]] for i in range(PEER_TOPK)]
        if _N_CAND_PAD > _N_CAND:
            cand.append(jnp.full((_N_CAND_PAD - _N_CAND, tt), -jnp.inf, F32))
        best = _extract_top(jnp.concatenate(cand, axis=0), PEER_TOPK + 1)
        zsum = sum(jnp.exp(bk - best[0]) for bk in best[:PEER_TOPK])
        mz = best[0] + jnp.log(zsum)
        tau = 0.5 * (best[PEER_TOPK - 1] + best[PEER_TOPK])
        s1_ref[0, p] = (sc[0] - mz) * LOG2E
        s2_ref[0, p] = sc[1] * LOG2E
        tau_ref[0, p:p + 1, :] = (tau - mz) * LOG2E


def peer_topk(q, sub_keys):
    t = q.shape[0]
    tt = PEER_TT
    nt = t // tt
    big = jax.ShapeDtypeStruct((nt, PEER_HEADS, N_KEYS, tt), F32)
    return pl.pallas_call(
        _peer_topk_kernel,
        grid=(nt,),
        in_specs=[pl.BlockSpec((tt, q.shape[1]), lambda i: (i, 0)),
                  pl.BlockSpec((2, N_KEYS, PEER_HALF), lambda i: (0, 0, 0))],
        out_specs=[pl.BlockSpec((1, PEER_HEADS, N_KEYS, tt), lambda i: (i, 0, 0, 0)),
                   pl.BlockSpec((1, PEER_HEADS, N_KEYS, tt), lambda i: (i, 0, 0, 0)),
                   pl.BlockSpec((1, PEER_HEADS, tt), lambda i: (i, 0, 0))],
        out_shape=[big, big, jax.ShapeDtypeStruct((nt, PEER_HEADS, tt), F32)],
        compiler_params=_cparams(("arbitrary",)),
        name="peer_topk",
    )(q, sub_keys)


def _peer_expert_kernel(h_ref, s1_ref, s2_ref, tau_ref, u_ref, vt_ref, x_ref, gate_ref, o_ref, acc,
                        act_scr, w_scr):
    e = pl.program_id(1)
    ne = pl.num_programs(1)
    tt = h_ref.shape[0]
    a_per = PEER_TE // N_KEYS
    gb = 32

    @pl.when(e == 0)
    def _():
        acc[...] = jnp.zeros_like(acc)

    act_scr[...] = lax.dot_general(u_ref[...], h_ref[...], (((1,), (1,)), ((), ())),
                                   preferred_element_type=F32)
    for ai in range(a_per):
        a = e * a_per + ai
        for tc in range(tt // LANES):
            cols = slice(tc * LANES, (tc + 1) * LANES)
            for bi in range(N_KEYS // gb):
                bs = slice(bi * gb, (bi + 1) * gb)
                rs = slice(ai * N_KEYS + bi * gb, ai * N_KEYS + (bi + 1) * gb)
                gsum = jnp.zeros((gb, LANES), F32)
                for p in range(PEER_HEADS):
                    s1_row = s1_ref[0, p, pl.ds(a, 1), :][:, cols]
                    tsum = s1_row + s2_ref[0, p, bs, cols]
                    gsum = gsum + jnp.where(tsum > tau_ref[0, p:p + 1, cols], jnp.exp2(tsum), 0.0)
                act = act_scr[rs, cols]
                gel = 0.5 * act * (1.0 + lax.erf(act * (2.0 ** -0.5)))
                w_scr[rs, cols] = (gsum * gel).astype(BF16)
    acc[...] += jnp.dot(vt_ref[...], w_scr[...], preferred_element_type=F32)

    @pl.when(e == ne - 1)
    def _():
        o_ref[...] = x_ref[...] + gate_ref[0] * acc[...].T


def peer_experts(h, s1, s2, tau, u_bf, vt_bf, x, gate, *, seq):
    t, d = h.shape
    n_exp = u_bf.shape[0]
    tt, te = PEER_TT, PEER_TE
    bidx = lambda i: (i * tt) // seq
    once = pl.Buffered(1)
    return pl.pallas_call(
        _peer_expert_kernel,
        grid=(t // tt, n_exp // te),
        in_specs=[pl.BlockSpec((tt, d), lambda i, e: (i, 0), pipeline_mode=once),
                  pl.BlockSpec((1, PEER_HEADS, N_KEYS, tt), lambda i, e: (i, 0, 0, 0),
                               pipeline_mode=once),
                  pl.BlockSpec((1, PEER_HEADS, N_KEYS, tt), lambda i, e: (i, 0, 0, 0),
                               pipeline_mode=once),
                  pl.BlockSpec((1, PEER_HEADS, tt), lambda i, e: (i, 0, 0)),
                  pl.BlockSpec((te, d), lambda i, e: (e, 0)),
                  pl.BlockSpec((d, te), lambda i, e: (0, e)),
                  pl.BlockSpec((tt, d), lambda i, e: (i, 0), pipeline_mode=once),
                  pl.BlockSpec((1, 1, d), lambda i, e: (bidx(i), 0, 0))],
        out_specs=pl.BlockSpec((tt, d), lambda i, e: (i, 0)),
        out_shape=jax.ShapeDtypeStruct((t, d), F32),
        scratch_shapes=[pltpu.VMEM((d, tt), F32),
                        pltpu.VMEM((te, tt), F32), pltpu.VMEM((te, tt), BF16)],
        compiler_params=_cparams(("arbitrary", "arbitrary")),
        name="peer_experts",
    )(h, s1, s2, tau, u_bf, vt_bf, x, gate)


def _rmsnorm_kernel(x_ref, g_ref, o_ref):
    x = x_ref[...]
    o_ref[...] = x * lax.rsqrt(jnp.mean(x * x, axis=-1, keepdims=True) + NORM_EPS) * g_ref[...]


def final_rmsnorm(x, g, tm=512):
    t, d = x.shape
    return pl.pallas_call(
        _rmsnorm_kernel,
        grid=(t // tm,),
        in_specs=[pl.BlockSpec((tm, d), lambda i: (i, 0)), pl.BlockSpec((1, d), lambda i: (0, 0))],
        out_specs=pl.BlockSpec((tm, d), lambda i: (i, 0)),
        out_shape=jax.ShapeDtypeStruct((t, d), F32),
        compiler_params=_cparams(("arbitrary",)),
        name="final_rmsnorm",
    )(x, g.reshape(1, d))


def _rope_tables(positions):
    d = DIFF_HEAD_DIM
    inv_freq = ROPE_THETA ** (-jnp.arange(0, d, 2, dtype=F32) / d)
    ang = positions.astype(F32)[..., None] * inv_freq
    cos, sin = jnp.cos(ang), jnp.sin(ang)
    cosf = jnp.concatenate([cos, cos], axis=-1).reshape(-1, d)
    sins = jnp.concatenate([-sin, sin], axis=-1).reshape(-1, d)
    return cosf, sins


def kernel(x, c, positions, ada_w, ada_b, norm_mix_g, norm_ffn_g, gdn_w_in, gdn_conv_w, gdn_a_log,
           gdn_dt_bias, gdn_o_gain, gdn_w_out, kv_norm_g, kv_ada_w, kv_ada_b, kv_w, diff_w_q,
           diff_lambda, diff_subln_g, diff_w_out, peer_w_q, peer_sub_keys, peer_u, peer_v, final_g):
    batch, seq, d = x.shape
    depth = ada_w.shape[0]
    t = batch * seq
    xs = x.reshape(t, d)
    cosf, sins = _rope_tables(positions)

    c_pad = jnp.pad(c, ((0, SUBLANES - batch), (0, 0)))
    mods = adaln_all(c_pad, ada_w, ada_b)[:, :batch]
    kv_mods = adaln_all(c_pad, kv_ada_w[None], kv_ada_b[None])[0, :batch]
    mod = lambda m: m[:, None, :]

    k_sh = vt_sh = None
    for l in range(depth):
        sh1, sc1, gt1, sh2, sc2, gt2 = [mod(m) for m in jnp.split(mods[l], 6, axis=-1)]
        norm1 = (norm_mix_g[l].reshape(1, d), sh1, sc1)
        if l < N_A_LAYERS:
            w_in = gdn_w_in[l]
            w4 = 4 * GDN_W
            proj = fused_matmul(xs, w_in[:, :w4].astype(BF16), seq=seq, norm=norm1, emit_h=True,
                                out_dtype=BF16)
            proj, h1 = proj
            w_ab = jnp.pad(w_in[:, w4:], ((0, 0), (0, LANES - 2 * GDN_HEADS))).astype(BF16)
            ab = fused_matmul(h1, w_ab, seq=seq)
            o = gdn_core(proj, ab[:, :GDN_HEADS],
                         ab[:, GDN_HEADS:2 * GDN_HEADS], gdn_conv_w[l], gdn_a_log[l],
                         gdn_dt_bias[l], gdn_o_gain[l], batch=batch, seq=seq)
            xs = fused_matmul(o, gdn_w_out[l].astype(BF16), seq=seq, res=(xs, gt1))
        else:
            j = l - N_A_LAYERS
            lambda_init = 0.8 - 0.6 * math.exp(-0.3 * l)
            lp = diff_lambda[j].astype(F32)
            lam = jnp.exp(jnp.sum(lp[0] * lp[1])) - jnp.exp(jnp.sum(lp[2] * lp[3])) + lambda_init
            qw = DIFF_HEADS * 2 * DIFF_HEAD_DIM
            q = fused_matmul(xs, diff_w_q[j].astype(BF16), seq=seq, norm=norm1,
                             rope=(cosf, sins, qw, DIFF_HEAD_DIM ** -0.5 * LOG2E), out_dtype=BF16)
            o = diff_attn_core(q, k_sh, vt_sh, lam.reshape(1), diff_subln_g[j], batch=batch,
                               seq=seq, out_scale=1.0 - lambda_init)
            xs = fused_matmul(o, diff_w_out[j].astype(BF16), seq=seq, res=(xs, gt1))

        norm2 = (norm_ffn_g[l].reshape(1, d), sh2, sc2)
        pq, h2 = fused_matmul(xs, peer_w_q[l].astype(BF16), seq=seq, norm=norm2, emit_h=True)
        s1, s2, tau = peer_topk(pq, peer_sub_keys[l])
        xs = peer_experts(h2, s1, s2, tau, peer_u[l].astype(BF16), peer_v[l].T.astype(BF16),
                          xs, gt2, seq=seq)

        if l == N_A_LAYERS - 1:
            kvsh, kvsc = [mod(m) for m in jnp.split(kv_mods, 2, axis=-1)]
            qw = DIFF_HEADS * 2 * DIFF_HEAD_DIM
            k_sh, vt_sh = fused_matmul(xs, kv_w.astype(BF16), seq=seq,
                                       norm=(kv_norm_g.reshape(1, d), kvsh, kvsc),
                                       rope=(cosf, sins, qw, 1.0), out_dtype=BF16,
                                       tail_hw=2 * DIFF_HEAD_DIM, tm=ATT_TK)

    return final_rmsnorm(xs, final_g).reshape(batch, seq, d)
```

```python
import functools
import math

import jax
import jax.numpy as jnp
from jax import lax
from jax.experimental import pallas as pl
from jax.experimental.pallas import tpu as pltpu

F32 = jnp.float32
BF16 = jnp.bfloat16

NORM_EPS = 1e-6
N_A_LAYERS = 2
GDN_HEADS = 16
GDN_HEAD_DIM = 128
CONV_K = 4
CHUNK = 64
DIFF_HEADS = 8
DIFF_HEAD_DIM = 128
ROPE_THETA = 10000.0
PEER_HEADS = 8
PEER_HALF = 128
N_KEYS = 128
PEER_TOPK = 16

LOG2E = math.log2(math.e)

LANES = 128
SUBLANES = 8
VMEM_LIMIT = 56 * 1024 * 1024


def _cparams(sem):
    return pltpu.CompilerParams(dimension_semantics=sem, vmem_limit_bytes=VMEM_LIMIT)


def _dot(a, b):
    return jnp.dot(a.astype(BF16), b.astype(BF16), preferred_element_type=F32)


def _dot_nt(a, b):
    return lax.dot_general(a.astype(BF16), b.astype(BF16), (((1,), (1,)), ((), ())),
                           preferred_element_type=F32)


def _dot_tn(a, b):
    return lax.dot_general(a.astype(BF16), b.astype(BF16), (((0,), (0,)), ((), ())),
                           preferred_element_type=F32)


def _split3(x):
    hi = x.astype(BF16)
    r1 = x - hi.astype(F32)
    mid = r1.astype(BF16)
    lo = (r1 - mid.astype(F32)).astype(BF16)
    return hi, mid, lo


def _sigmoid(x):
    return 1.0 / (1.0 + jnp.exp(-x))


def _softplus(x):
    return jnp.maximum(x, 0.0) + jnp.log(1.0 + jnp.exp(-jnp.abs(x)))


def _adaln_kernel(c_ref, w_ref, b_ref, o_ref):
    c = c_ref[...]
    s = c * _sigmoid(c)
    o_ref[0] = _dot(s, w_ref[0]) + b_ref[0]


def adaln_all(c_pad, w, b, tn=1024):
    nl, d, n = w.shape
    rows = c_pad.shape[0]
    return pl.pallas_call(
        _adaln_kernel,
        grid=(nl, n // tn),
        in_specs=[pl.BlockSpec((rows, d), lambda l, j: (0, 0)),
                  pl.BlockSpec((1, d, tn), lambda l, j: (l, 0, j)),
                  pl.BlockSpec((1, 1, tn), lambda l, j: (l, 0, j))],
        out_specs=pl.BlockSpec((1, rows, tn), lambda l, j: (l, 0, j)),
        out_shape=jax.ShapeDtypeStruct((nl, rows, n), F32),
        compiler_params=_cparams(("arbitrary", "arbitrary")),
        name="adaln",
    )(c_pad, w, b.reshape(nl, 1, n))


def _rope_tile(acc, cosf, sins, qscale):
    pieces = []
    for g in range(acc.shape[1] // LANES):
        xg = acc[:, g * LANES:(g + 1) * LANES]
        rot = pltpu.roll(xg, LANES // 2, axis=1)
        pieces.append((xg * cosf + rot * sins) * qscale)
    return jnp.concatenate(pieces, axis=1) if len(pieces) > 1 else pieces[0]


def _mm_kernel(*refs, has_norm, mode, emit_h, rope_tiles, qscale, tail_hw):
    it = iter(refs)
    x_ref = next(it)
    if has_norm:
        g_ref, sh_ref, sc_ref = next(it), next(it), next(it)
    w_ref = next(it)
    if mode == "rope":
        cos_ref, sin_ref = next(it), next(it)
    if mode == "res":
        res_ref, gate_ref = next(it), next(it)
    o_ref = next(it)
    if emit_h:
        hout_ref = next(it)
    if tail_hw:
        tail_ref = next(it)
    if has_norm:
        h_scr = next(it)
    j = pl.program_id(1)

    if has_norm:
        @pl.when(j == 0)
        def _():
            x = x_ref[...]
            ms = jnp.mean(x * x, axis=-1, keepdims=True)
            y = x * lax.rsqrt(ms + NORM_EPS) * g_ref[...]
            h = (y * (1.0 + sc_ref[0]) + sh_ref[0]).astype(BF16)
            h_scr[...] = h
            if emit_h:
                hout_ref[...] = h
        lhs = h_scr[...]
    else:
        lhs = x_ref[...]

    acc = jnp.dot(lhs, w_ref[...], preferred_element_type=F32)
    if mode == "rope":
        @pl.when(j < rope_tiles)
        def _():
            o_ref[...] = _rope_tile(acc, cos_ref[...], sin_ref[...], qscale).astype(o_ref.dtype)

        if tail_hw:
            @pl.when(j >= rope_tiles)
            def _():
                tn, tm = acc.shape[1], acc.shape[0]
                tail_ref[0, :, 0] = acc.T.reshape(tn // tail_hw, tail_hw, tm).astype(tail_ref.dtype)
    elif mode == "res":
        o_ref[...] = (res_ref[...] + gate_ref[0] * acc).astype(o_ref.dtype)
    else:
        o_ref[...] = acc.astype(o_ref.dtype)


def fused_matmul(x, w, *, seq, norm=None, rope=None, res=None, out_dtype=F32, emit_h=False,
                 tail_hw=0, tm=1024, tn=512):
    t, k = x.shape
    n = w.shape[1]
    tn = min(tn, n)
    assert t % tm == 0 and n % tn == 0 and seq % tm == 0
    bidx = lambda i: (i * tm) // seq
    in_specs = [pl.BlockSpec((tm, k), lambda i, j: (i, 0))]
    args = [x]
    if norm is not None:
        g, sh, sc = norm
        in_specs += [pl.BlockSpec((1, k), lambda i, j: (0, 0)),
                     pl.BlockSpec((1, 1, k), lambda i, j: (bidx(i), 0, 0)),
                     pl.BlockSpec((1, 1, k), lambda i, j: (bidx(i), 0, 0))]
        args += [g, sh, sc]
    in_specs.append(pl.BlockSpec((k, tn), lambda i, j: (0, j)))
    args.append(w)
    mode, rope_tiles, qscale = "none", 0, 1.0
    if rope is not None:
        cosf, sins, rope_cols, qscale = rope
        mode, rope_tiles = "rope", rope_cols // tn
        in_specs += [pl.BlockSpec((tm, LANES), lambda i, j: (i, 0)),
                     pl.BlockSpec((tm, LANES), lambda i, j: (i, 0))]
        args += [cosf, sins]
    if res is not None:
        r, gate = res
        mode = "res"
        in_specs += [pl.BlockSpec((tm, tn), lambda i, j: (i, j)),
                     pl.BlockSpec((1, 1, tn), lambda i, j: (bidx(i), 0, j))]
        args += [r, gate]
    out_specs = [pl.BlockSpec((tm, tn), lambda i, j: (i, j))]
    out_shape = [jax.ShapeDtypeStruct((t, n), out_dtype)]
    if emit_h:
        out_specs.append(pl.BlockSpec((tm, k), lambda i, j: (i, 0)))
        out_shape.append(jax.ShapeDtypeStruct((t, k), BF16))
    if mode == "rope" and rope_cols < n:
        assert tail_hw and tn % tail_hw == 0 and (n - rope_cols) % tn == 0
        rpb = seq // tm
        out_specs[0] = pl.BlockSpec((tm, tn), lambda i, j: (i, jnp.minimum(j, rope_tiles - 1)))
        out_shape[0] = jax.ShapeDtypeStruct((t, rope_cols), out_dtype)
        out_specs.append(pl.BlockSpec(
            (1, tn // tail_hw, 1, tail_hw, tm),
            lambda i, j: (bidx(i), jnp.maximum(j - rope_tiles, 0), i - bidx(i) * rpb, 0, 0)))
        out_shape.append(jax.ShapeDtypeStruct(
            (t // seq, (n - rope_cols) // tail_hw, rpb, tail_hw, tm), out_dtype))
    else:
        tail_hw = 0
    scratch = [pltpu.VMEM((tm, k), BF16)] if norm is not None else []
    outs = pl.pallas_call(
        functools.partial(_mm_kernel, has_norm=norm is not None, mode=mode, emit_h=emit_h,
                          rope_tiles=rope_tiles, qscale=qscale, tail_hw=tail_hw),
        grid=(t // tm, n // tn),
        in_specs=in_specs, out_specs=out_specs, out_shape=out_shape,
        scratch_shapes=scratch,
        compiler_params=_cparams(("arbitrary", "arbitrary")),
        name="fused_mm_" + mode,
    )(*args)
    return outs if len(outs) > 1 else outs[0]


GDN_ROWS = 256
GDN_W = GDN_HEADS * GDN_HEAD_DIM


def _gdn_kernel(qkv_ref, z_ref, a_ref, b_ref, at_ref, convw_ref, alog_ref, dtb_ref, alogc_ref,
                dtbc_ref, ogain_ref, o_ref, xbuf, qkvs, qn_s, kn_s, kb_s, qg_s, kd_s, kbgv_s, dec_s,
                l_s, a_s, tinv_s, tinvb_s, t1_s, w_s, u_s, unew_s, state, stateb):
    r = pl.program_id(1)
    rows = GDN_ROWS
    hd = GDN_HEAD_DIM
    nh = GDN_HEADS
    ncs = rows // CHUNK
    units = [(ci, h) for ci in range(ncs) for h in range(nh)]
    rsl = lambda ci: slice(ci * CHUNK, (ci + 1) * CHUNK)
    csl = lambda h: slice(h * hd, (h + 1) * hd)

    @pl.when(r == 0)
    def _():
        state[...] = jnp.zeros_like(state)
        stateb[...] = jnp.zeros_like(stateb)
        xbuf[0:SUBLANES, :] = jnp.zeros((SUBLANES, 3 * GDN_W), F32)

    cb = 512
    for c in range(3 * GDN_W // cb):
        cs = slice(c * cb, (c + 1) * cb)
        xbuf[SUBLANES:SUBLANES + rows, cs] = qkv_ref[:, cs].astype(F32)
        acc = convw_ref[3:4, cs] * xbuf[SUBLANES:SUBLANES + rows, cs]
        for i in range(CONV_K - 1):
            off = SUBLANES - (CONV_K - 1) + i
            acc = acc + convw_ref[i:i + 1, cs] * xbuf[off:off + rows, cs]
        qkvs[:, cs] = acc * _sigmoid(acc)
        xbuf[0:SUBLANES, cs] = xbuf[rows:rows + SUBLANES, cs]

    neg_a = -jnp.exp(alog_ref[...])
    g_all = neg_a * _softplus(a_ref[...] + dtb_ref[...])
    beta_all = _sigmoid(b_ref[...])
    neg_ac = -jnp.exp(alogc_ref[...])

    ii = lax.broadcasted_iota(jnp.int32, (CHUNK, CHUNK), 0)
    jj = lax.broadcasted_iota(jnp.int32, (CHUNK, CHUNK), 1)
    strict = ii > jj
    eye = ii == jj
    tril = jnp.where(ii >= jj, 1.0, 0.0).astype(BF16)
    triu = jnp.where(ii <= jj, 1.0, 0.0).astype(BF16)
    eye_f = jnp.where(eye, 1.0, 0.0).astype(F32)
    sib_masks = []
    for m in range(CHUNK.bit_length() - 1):
        sib_masks.append(((ii >> (m + 1)) == (jj >> (m + 1))) & ((ii >> m) != (jj >> m)) & strict)
    ogain = ogain_ref[...]

    glast = []
    for ci in range(ncs):
        rs = rsl(ci)
        g = g_all[rs]
        beta = beta_all[rs]
        gt = neg_ac * _softplus(at_ref[0, ci] + dtbc_ref[...])
        G = sum(jnp.dot(tril, p, preferred_element_type=F32) for p in _split3(g))
        GT = sum(jnp.dot(p, triu, preferred_element_type=F32) for p in _split3(gt))
        eG = jnp.exp(G)
        eGl = jnp.exp(G[CHUNK - 1:CHUNK, :] - G)
        glast.append(jnp.exp(G[CHUNK - 1:CHUNK, :]))
        for h in range(nh):
            cs = csl(h)
            q = qkvs[rs, cs]
            k = qkvs[rs, GDN_W + h * hd:GDN_W + (h + 1) * hd]
            v = qkvs[rs, 2 * GDN_W + h * hd:2 * GDN_W + (h + 1) * hd]
            q = q * lax.rsqrt(jnp.sum(q * q, axis=-1, keepdims=True) + NORM_EPS) * (hd ** -0.5)
            k = k * lax.rsqrt(jnp.sum(k * k, axis=-1, keepdims=True) + NORM_EPS)
            beta_c = beta[:, h:h + 1]
            eG_c = eG[:, h:h + 1]
            kb = k * beta_c
            qn_s[rs, cs] = q.astype(BF16)
            kn_s[rs, cs] = k.astype(BF16)
            kb_s[rs, cs] = kb.astype(BF16)
            qg_s[rs, cs] = (q * eG_c).astype(BF16)
            kd_s[rs, cs] = (k * eGl[:, h:h + 1]).astype(BF16)
            kbgv_s[rs, 2 * h * hd:(2 * h + 1) * hd] = (kb * eG_c).astype(BF16)
            kbgv_s[rs, (2 * h + 1) * hd:(2 * h + 2) * hd] = (v * beta_c).astype(BF16)
            diff = G[:, h:h + 1] - GT[h:h + 1, :]
            dec_s[ci * nh + h] = jnp.exp(jnp.where(strict, diff, -jnp.inf))

    nt = lambda a, b: lax.dot_general(a, b, (((1,), (1,)), ((), ())), preferred_element_type=F32)
    mm = lambda a, b: jnp.dot(a, b, preferred_element_type=F32)

    for u, (ci, h) in enumerate(units):
        rs, cs = rsl(ci), csl(h)
        dec = dec_s[u]
        kn = kn_s[rs, cs]
        lmat = nt(kb_s[rs, cs], kn) * dec
        l_s[u] = lmat
        t0 = eye_f - jnp.where(sib_masks[0], lmat, 0.0)
        tinv_s[u] = t0
        tinvb_s[u] = t0.astype(BF16)
        a_s[u] = (nt(qn_s[rs, cs], kn) * jnp.where(eye, 1.0, dec)).astype(BF16)

    for sm in sib_masks[1:]:
        for u in range(len(units)):
            t1_s[u] = mm(jnp.where(sm, l_s[u], 0.0).astype(BF16), tinvb_s[u]).astype(BF16)
        for u in range(len(units)):
            tn = tinv_s[u] - mm(tinvb_s[u], t1_s[u])
            tinv_s[u] = tn
            tinvb_s[u] = tn.astype(BF16)

    for u, (ci, h) in enumerate(units):
        rs, cs = rsl(ci), csl(h)
        wu = mm(tinvb_s[u], kbgv_s[rs, 2 * h * hd:(2 * h + 2) * hd])
        w_s[rs, cs] = wu[:, :hd].astype(BF16)
        u_s[rs, cs] = wu[:, hd:]

    for ci in range(ncs):
        rs = rsl(ci)
        for h in range(nh):
            cs = csl(h)
            unew_s[h] = (u_s[rs, cs] - mm(w_s[rs, cs], stateb[h])).astype(BF16)
        for h in range(nh):
            cs = csl(h)
            un = unew_s[h]
            o = mm(qg_s[rs, cs], stateb[h]) + mm(a_s[ci * nh + h], un)
            snew = glast[ci][:, h:h + 1] * state[h] + lax.dot_general(
                kd_s[rs, cs], un, (((0,), (0,)), ((), ())), preferred_element_type=F32)
            state[h] = snew
            stateb[h] = snew.astype(BF16)
            o = o * lax.rsqrt(jnp.mean(o * o, axis=-1, keepdims=True) + NORM_EPS) * ogain
            zh = z_ref[rs, cs].astype(F32)
            o_ref[rs, cs] = (o * (zh * _sigmoid(zh))).astype(o_ref.dtype)


def gdn_core(proj, a, b, conv_w, a_log, dt_bias, o_gain, *, batch, seq):
    t = proj.shape[0]
    rows = GDN_ROWS
    nr = seq // rows
    cpr = rows // CHUNK
    h = GDN_HEADS
    a_t = a.reshape(batch, seq // CHUNK, CHUNK, h).transpose(0, 1, 3, 2)
    a_t = a_t.reshape(batch * nr, cpr, h, CHUNK)
    row_map = lambda bi, r: (bi * nr + r, 0)
    full = lambda bi, r: (0, 0)
    return pl.pallas_call(
        _gdn_kernel,
        grid=(batch, nr),
        in_specs=[pl.BlockSpec((rows, 3 * GDN_W), row_map),
                  pl.BlockSpec((rows, GDN_W), lambda bi, r: (bi * nr + r, 3)),
                  pl.BlockSpec((rows, h), row_map),
                  pl.BlockSpec((rows, h), row_map),
                  pl.BlockSpec((1, cpr, h, CHUNK), lambda bi, r: (bi * nr + r, 0, 0, 0)),
                  pl.BlockSpec((CONV_K, 3 * GDN_W), full),
                  pl.BlockSpec((1, h), full), pl.BlockSpec((1, h), full),
                  pl.BlockSpec((h, 1), full), pl.BlockSpec((h, 1), full),
                  pl.BlockSpec((1, GDN_HEAD_DIM), full)],
        out_specs=pl.BlockSpec((rows, GDN_W), row_map),
        out_shape=jax.ShapeDtypeStruct((t, GDN_W), BF16),
        scratch_shapes=[pltpu.VMEM((SUBLANES + rows, 3 * GDN_W), F32),
                        pltpu.VMEM((rows, 3 * GDN_W), F32),
                        pltpu.VMEM((rows, GDN_W), BF16),
                        pltpu.VMEM((rows, GDN_W), BF16),
                        pltpu.VMEM((rows, GDN_W), BF16),
                        pltpu.VMEM((rows, GDN_W), BF16),
                        pltpu.VMEM((rows, GDN_W), BF16),
                        pltpu.VMEM((rows, 2 * GDN_W), BF16),
                        pltpu.VMEM((cpr * h, CHUNK, CHUNK), F32),
                        pltpu.VMEM((cpr * h, CHUNK, CHUNK), F32),
                        pltpu.VMEM((cpr * h, CHUNK, CHUNK), BF16),
                        pltpu.VMEM((cpr * h, CHUNK, CHUNK), F32),
                        pltpu.VMEM((cpr * h, CHUNK, CHUNK), BF16),
                        pltpu.VMEM((cpr * h, CHUNK, CHUNK), BF16),
                        pltpu.VMEM((rows, GDN_W), BF16),
                        pltpu.VMEM((rows, GDN_W), F32),
                        pltpu.VMEM((h, CHUNK, GDN_HEAD_DIM), BF16),
                        pltpu.VMEM((h, GDN_HEAD_DIM, GDN_HEAD_DIM), F32),
                        pltpu.VMEM((h, GDN_HEAD_DIM, GDN_HEAD_DIM), BF16)],
        compiler_params=_cparams(("arbitrary", "arbitrary")),
        name="gdn_core",
    )(proj, proj, a, b, a_t, conv_w, a_log.reshape(1, h), dt_bias.reshape(1, h),
      a_log.reshape(h, 1), dt_bias.reshape(h, 1), o_gain.reshape(1, GDN_HEAD_DIM))


ATT_TQ = 1024
ATT_TK = 1024


def _attn_kernel(lam_ref, q_ref, k_ref, vt_ref, g_ref, o_ref, m_scr, l_scr, acc_scr, *, out_scale):
    i = pl.program_id(2)
    d = DIFF_HEAD_DIM
    tq, tk = ATT_TQ, ATT_TK
    q = q_ref[...]
    qs = [q[:, c * d:(c + 1) * d] for c in range(2)]
    m_scr[...] = jnp.full(m_scr.shape, -jnp.inf, F32)
    l_scr[...] = jnp.zeros(l_scr.shape, F32)
    acc_scr[...] = jnp.zeros(acc_scr.shape, F32)

    def tile(j, masked):
        k = k_ref[pl.ds(pl.multiple_of(j * tk, tk), tk), :]
        vt = vt_ref[0, 0, j]
        sts = [lax.dot_general(k[:, c * d:(c + 1) * d], qs[c], (((1,), (1,)), ((), ())),
                               preferred_element_type=F32) for c in range(2)]
        if masked:
            kpos = lax.broadcasted_iota(jnp.int32, (tk, tq), 0)
            qpos = lax.broadcasted_iota(jnp.int32, (tk, tq), 1)
            sts = [jnp.where(kpos <= qpos, st, -jnp.inf) for st in sts]
        ps, alphas = [], []
        for c in range(2):
            m_prev = m_scr[c]
            m_new = jnp.maximum(m_prev, jnp.max(sts[c], axis=0, keepdims=True))
            alpha = jnp.exp2(m_prev - m_new)
            p = jnp.exp2(sts[c] - m_new)
            l_scr[c] = alpha * l_scr[c] + jnp.sum(p, axis=0, keepdims=True)
            m_scr[c] = m_new
            ps.append(p.astype(BF16))
            alphas.append(alpha)
        for c in range(2):
            acc_scr[c] = alphas[c] * acc_scr[c] + jnp.dot(vt, ps[c],
                                                          preferred_element_type=F32)

    def body(j, carry):
        tile(j, False)
        return carry

    lax.fori_loop(0, i, body, 0)
    tile(i, True)

    lam = lam_ref[0]
    o = acc_scr[0] / l_scr[0] - lam * (acc_scr[1] / l_scr[1])
    o = o * lax.rsqrt(jnp.mean(o * o, axis=0, keepdims=True) + NORM_EPS) * (g_ref[...] * out_scale)
    o_ref[...] = o.T.astype(o_ref.dtype)


def diff_attn_core(q, k, vt, lam, subln_g, *, batch, seq, out_scale):
    t, width = q.shape
    hw = 2 * DIFF_HEAD_DIM
    assert ATT_TQ == ATT_TK
    nq, nk = seq // ATT_TQ, seq // ATT_TK
    assert vt.shape == (batch, DIFF_HEADS, nk, hw, ATT_TK)
    return pl.pallas_call(
        functools.partial(_attn_kernel, out_scale=out_scale),
        grid=(batch, DIFF_HEADS, nq),
        in_specs=[pl.BlockSpec(memory_space=pltpu.SMEM),
                  pl.BlockSpec((ATT_TQ, hw), lambda b, h, i: (b * nq + i, h)),
                  pl.BlockSpec((seq, hw), lambda b, h, i: (b, h)),
                  pl.BlockSpec((1, 1, nk, hw, ATT_TK), lambda b, h, i: (b, h, 0, 0, 0)),
                  pl.BlockSpec((hw, 1), lambda b, h, i: (0, 0))],
        out_specs=pl.BlockSpec((ATT_TQ, hw), lambda b, h, i: (b * nq + i, h)),
        out_shape=jax.ShapeDtypeStruct((t, width), BF16),
        scratch_shapes=[pltpu.VMEM((2, 1, ATT_TQ), F32), pltpu.VMEM((2, 1, ATT_TQ), F32),
                        pltpu.VMEM((2, hw, ATT_TQ), F32)],
        compiler_params=_cparams(("arbitrary", "arbitrary", "arbitrary")),
        name="diff_attn",
    )(lam, q, k, vt, subln_g.reshape(hw, 1))


PEER_TT = 512
PEER_TE = 1024
_CAND_COUNTS = tuple(min(PEER_TOPK, (PEER_TOPK + 1) // (i + 1)) for i in range(PEER_TOPK))
_N_CAND = sum(_CAND_COUNTS)
_N_CAND_PAD = -(-_N_CAND // SUBLANES) * SUBLANES


def _extract_top(x, n):
    rows = x.shape[0]
    iota = lax.broadcasted_iota(jnp.int32, x.shape, 0)
    vals = []
    for _ in range(n):
        m = jnp.max(x, axis=0, keepdims=True)
        idx = jnp.min(jnp.where(x == m, iota, rows), axis=0, keepdims=True)
        x = jnp.where(iota == idx, -jnp.inf, x)
        vals.append(m)
    return vals


def _oddeven_merge_sort_pairs(n):
    pairs = []
    p = 1
    while p < n:
        k = p
        while k >= 1:
            for j in range(k % p, n - k, 2 * k):
                for i in range(min(k, n - j - k)):
                    if (i + j) // (2 * p) == (i + j + k) // (2 * p):
                        pairs.append((i + j, i + j + k))
            k //= 2
        p *= 2
    return pairs


_SORT16 = tuple(_oddeven_merge_sort_pairs(PEER_TOPK))


def _top16_sorted(x):
    n = PEER_TOPK
    vs = [x[SUBLANES * r:SUBLANES * (r + 1), :] for r in range(n)]
    for i, j in _SORT16:
        vs[i], vs[j] = jnp.maximum(vs[i], vs[j]), jnp.minimum(vs[i], vs[j])
    shift = SUBLANES // 2
    while shift >= 1:
        vs = [jnp.maximum(vs[i], pltpu.roll(vs[n - 1 - i], shift, axis=0)) for i in range(n)]
        k = n // 2
        while k >= 1:
            for i in range(n):
                if i & k == 0:
                    vs[i], vs[i + k] = jnp.maximum(vs[i], vs[i + k]), jnp.minimum(vs[i], vs[i + k])
            k //= 2
        shift //= 2
    return vs


def _peer_topk_kernel(q_ref, keys_ref, s1_ref, s2_ref, tau_ref):
    tt = q_ref.shape[0]
    kparts = [_split3(keys_ref[j])[:2] for j in range(2)]
    for p in range(PEER_HEADS):
        sc = []
        for j in range(2):
            c0 = (2 * p + j) * PEER_HALF
            qh, ql = _split3(q_ref[:, c0:c0 + PEER_HALF])[:2]
            kh, kl = kparts[j]
            nt = lambda a, b: lax.dot_general(a, b, (((1,), (1,)), ((), ())),
                                              preferred_element_type=F32)
            sc.append(nt(kh, qh) + (nt(kh, ql) + nt(kl, qh)))
        v1 = [v[0:1, :] for v in _top16_sorted(sc[0])]
        v2 = jnp.concatenate([v[0:1, :] for v in _top16_sorted(sc[1])], axis=0)
        cand = [v1[i] + v2[0:_CAND_COUNTS[i]] for i in range(PEER_TOPK)]
        if _N_CAND_PAD > _N_CAND:
            cand.append(jnp.full((_N_CAND_PAD - _N_CAND, tt), -jnp.inf, F32))
        best = _extract_top(jnp.concatenate(cand, axis=0), PEER_TOPK + 1)
        zsum = sum(jnp.exp(bk - best[0]) for bk in best[:PEER_TOPK])
        mz = best[0] + jnp.log(zsum)
        tau = 0.5 * (best[PEER_TOPK - 1] + best[PEER_TOPK])
        s1_ref[0, p] = (sc[0] - mz) * LOG2E
        s2_ref[0, p] = sc[1] * LOG2E
        tau_ref[0, p:p + 1, :] = (tau - mz) * LOG2E


def peer_topk(q, sub_keys):
    t = q.shape[0]
    tt = PEER_TT
    nt = t // tt
    big = jax.ShapeDtypeStruct((nt, PEER_HEADS, N_KEYS, tt), F32)
    return pl.pallas_call(
        _peer_topk_kernel,
        grid=(nt,),
        in_specs=[pl.BlockSpec((tt, q.shape[1]), lambda i: (i, 0)),
                  pl.BlockSpec((2, N_KEYS, PEER_HALF), lambda i: (0, 0, 0))],
        out_specs=[pl.BlockSpec((1, PEER_HEADS, N_KEYS, tt), lambda i: (i, 0, 0, 0)),
                   pl.BlockSpec((1, PEER_HEADS, N_KEYS, tt), lambda i: (i, 0, 0, 0)),
                   pl.BlockSpec((1, PEER_HEADS, tt), lambda i: (i, 0, 0))],
        out_shape=[big, big, jax.ShapeDtypeStruct((nt, PEER_HEADS, tt), F32)],
        compiler_params=_cparams(("arbitrary",)),
        name="peer_topk",
    )(q, sub_keys)


def _peer_expert_kernel(h_ref, s1_ref, s2_ref, tau_ref, u_ref, vt_ref, x_ref, gate_ref, o_ref, acc,
                        act_scr, w_scr):
    e = pl.program_id(1)
    ne = pl.num_programs(1)
    tt = h_ref.shape[0]
    a_per = PEER_TE // N_KEYS
    gb = 32

    @pl.when(e == 0)
    def _():
        acc[...] = jnp.zeros_like(acc)

    act_scr[...] = lax.dot_general(u_ref[0], h_ref[...], (((1,), (1,)), ((), ())),
                                   preferred_element_type=F32)
    for ai in range(a_per):
        a = e * a_per + ai
        for tc in range(tt // LANES):
            cols = slice(tc * LANES, (tc + 1) * LANES)
            for bi in range(N_KEYS // gb):
                bs = slice(bi * gb, (bi + 1) * gb)
                rs = slice(ai * N_KEYS + bi * gb, ai * N_KEYS + (bi + 1) * gb)
                gsum = jnp.zeros((gb, LANES), F32)
                for p in range(PEER_HEADS):
                    s1_row = s1_ref[0, p, pl.ds(a, 1), :][:, cols]
                    tsum = s1_row + s2_ref[0, p, bs, cols]
                    gsum = gsum + jnp.where(tsum > tau_ref[0, p:p + 1, cols], jnp.exp2(tsum), 0.0)
                act = act_scr[rs, cols]
                gel = 0.5 * act * (1.0 + lax.erf(act * (2.0 ** -0.5)))
                w_scr[rs, cols] = (gsum * gel).astype(BF16)
    acc[...] += jnp.dot(vt_ref[0], w_scr[...], preferred_element_type=F32)

    @pl.when(e == ne - 1)
    def _():
        o_ref[...] = x_ref[...] + gate_ref[0] * acc[...].T


def peer_experts(h, s1, s2, tau, u_bf, vt_bf, x, gate, *, layer, seq):
    t, d = h.shape
    n_exp = u_bf.shape[1]
    tt, te = PEER_TT, PEER_TE
    bidx = lambda i: (i * tt) // seq
    once = pl.Buffered(1)
    return pl.pallas_call(
        _peer_expert_kernel,
        grid=(t // tt, n_exp // te),
        in_specs=[pl.BlockSpec((tt, d), lambda i, e: (i, 0), pipeline_mode=once),
                  pl.BlockSpec((1, PEER_HEADS, N_KEYS, tt), lambda i, e: (i, 0, 0, 0),
                               pipeline_mode=once),
                  pl.BlockSpec((1, PEER_HEADS, N_KEYS, tt), lambda i, e: (i, 0, 0, 0),
                               pipeline_mode=once),
                  pl.BlockSpec((1, PEER_HEADS, tt), lambda i, e: (i, 0, 0)),
                  pl.BlockSpec((1, te, d), lambda i, e: (layer, e, 0)),
                  pl.BlockSpec((1, d, te), lambda i, e: (layer, 0, e)),
                  pl.BlockSpec((tt, d), lambda i, e: (i, 0), pipeline_mode=once),
                  pl.BlockSpec((1, 1, d), lambda i, e: (bidx(i), 0, 0))],
        out_specs=pl.BlockSpec((tt, d), lambda i, e: (i, 0)),
        out_shape=jax.ShapeDtypeStruct((t, d), F32),
        scratch_shapes=[pltpu.VMEM((d, tt), F32),
                        pltpu.VMEM((te, tt), F32), pltpu.VMEM((te, tt), BF16)],
        compiler_params=_cparams(("arbitrary", "arbitrary")),
        name="peer_experts",
    )(h, s1, s2, tau, u_bf, vt_bf, x, gate)


def _rmsnorm_kernel(x_ref, g_ref, o_ref):
    x = x_ref[...]
    o_ref[...] = x * lax.rsqrt(jnp.mean(x * x, axis=-1, keepdims=True) + NORM_EPS) * g_ref[...]


def final_rmsnorm(x, g, tm=512):
    t, d = x.shape
    return pl.pallas_call(
        _rmsnorm_kernel,
        grid=(t // tm,),
        in_specs=[pl.BlockSpec((tm, d), lambda i: (i, 0)), pl.BlockSpec((1, d), lambda i: (0, 0))],
        out_specs=pl.BlockSpec((tm, d), lambda i: (i, 0)),
        out_shape=jax.ShapeDtypeStruct((t, d), F32),
        compiler_params=_cparams(("arbitrary",)),
        name="final_rmsnorm",
    )(x, g.reshape(1, d))


def _rope_tables(positions):
    d = DIFF_HEAD_DIM
    inv_freq = ROPE_THETA ** (-jnp.arange(0, d, 2, dtype=F32) / d)
    ang = positions.astype(F32)[..., None] * inv_freq
    cos, sin = jnp.cos(ang), jnp.sin(ang)
    cosf = jnp.concatenate([cos, cos], axis=-1).reshape(-1, d)
    sins = jnp.concatenate([-sin, sin], axis=-1).reshape(-1, d)
    return cosf, sins


def kernel(x, c, positions, ada_w, ada_b, norm_mix_g, norm_ffn_g, gdn_w_in, gdn_conv_w, gdn_a_log,
           gdn_dt_bias, gdn_o_gain, gdn_w_out, kv_norm_g, kv_ada_w, kv_ada_b, kv_w, diff_w_q,
           diff_lambda, diff_subln_g, diff_w_out, peer_w_q, peer_sub_keys, peer_u, peer_v, final_g):
    batch, seq, d = x.shape
    depth = ada_w.shape[0]
    t = batch * seq
    xs = x.reshape(t, d)
    cosf, sins = _rope_tables(positions)

    c_pad = jnp.pad(c, ((0, SUBLANES - batch), (0, 0)))
    mods = adaln_all(c_pad, ada_w, ada_b)[:, :batch]
    kv_mods = adaln_all(c_pad, kv_ada_w[None], kv_ada_b[None])[0, :batch]
    mod = lambda m: m[:, None, :]

    u_all = peer_u.astype(BF16)
    vt_all = jnp.swapaxes(peer_v, 1, 2).astype(BF16)

    k_sh = vt_sh = None
    for l in range(depth):
        sh1, sc1, gt1, sh2, sc2, gt2 = [mod(m) for m in jnp.split(mods[l], 6, axis=-1)]
        norm1 = (norm_mix_g[l].reshape(1, d), sh1, sc1)
        if l < N_A_LAYERS:
            w_in = gdn_w_in[l]
            w4 = 4 * GDN_W
            proj = fused_matmul(xs, w_in[:, :w4].astype(BF16), seq=seq, norm=norm1, emit_h=True,
                                out_dtype=BF16)
            proj, h1 = proj
            w_ab = jnp.pad(w_in[:, w4:], ((0, 0), (0, LANES - 2 * GDN_HEADS))).astype(BF16)
            ab = fused_matmul(h1, w_ab, seq=seq)
            o = gdn_core(proj, ab[:, :GDN_HEADS],
                         ab[:, GDN_HEADS:2 * GDN_HEADS], gdn_conv_w[l], gdn_a_log[l],
                         gdn_dt_bias[l], gdn_o_gain[l], batch=batch, seq=seq)
            xs = fused_matmul(o, gdn_w_out[l].astype(BF16), seq=seq, res=(xs, gt1))
        else:
            j = l - N_A_LAYERS
            lambda_init = 0.8 - 0.6 * math.exp(-0.3 * l)
            lp = diff_lambda[j].astype(F32)
            lam = jnp.exp(jnp.sum(lp[0] * lp[1])) - jnp.exp(jnp.sum(lp[2] * lp[3])) + lambda_init
            qw = DIFF_HEADS * 2 * DIFF_HEAD_DIM
            q = fused_matmul(xs, diff_w_q[j].astype(BF16), seq=seq, norm=norm1,
                             rope=(cosf, sins, qw, DIFF_HEAD_DIM ** -0.5 * LOG2E), out_dtype=BF16)
            o = diff_attn_core(q, k_sh, vt_sh, lam.reshape(1), diff_subln_g[j], batch=batch,
                               seq=seq, out_scale=1.0 - lambda_init)
            xs = fused_matmul(o, diff_w_out[j].astype(BF16), seq=seq, res=(xs, gt1))

        norm2 = (norm_ffn_g[l].reshape(1, d), sh2, sc2)
        pq, h2 = fused_matmul(xs, peer_w_q[l].astype(BF16), seq=seq, norm=norm2, emit_h=True)
        s1, s2, tau = peer_topk(pq, peer_sub_keys[l])
        xs = peer_experts(h2, s1, s2, tau, u_all, vt_all, xs, gt2, layer=l, seq=seq)

        if l == N_A_LAYERS - 1:
            kvsh, kvsc = [mod(m) for m in jnp.split(kv_mods, 2, axis=-1)]
            qw = DIFF_HEADS * 2 * DIFF_HEAD_DIM
            k_sh, vt_sh = fused_matmul(xs, kv_w.astype(BF16), seq=seq,
                                       norm=(kv_norm_g.reshape(1, d), kvsh, kvsc),
                                       rope=(cosf, sins, qw, 1.0), out_dtype=BF16,
                                       tail_hw=2 * DIFF_HEAD_DIM, tm=ATT_TK)

    return final_rmsnorm(xs, final_g).reshape(batch, seq, d)
```

```python
import functools
import math

import jax
import jax.numpy as jnp
from jax import lax
from jax.experimental import pallas as pl
from jax.experimental.pallas import tpu as pltpu

F32 = jnp.float32
BF16 = jnp.bfloat16

NORM_EPS = 1e-6
N_A_LAYERS = 2
GDN_HEADS = 16
GDN_HEAD_DIM = 128
CONV_K = 4
CHUNK = 64
DIFF_HEADS = 8
DIFF_HEAD_DIM = 128
ROPE_THETA = 10000.0
PEER_HEADS = 8
PEER_HALF = 128
N_KEYS = 128
PEER_TOPK = 16

LOG2E = math.log2(math.e)

LANES = 128
SUBLANES = 8
VMEM_LIMIT = 56 * 1024 * 1024


def _cparams(sem):
    return pltpu.CompilerParams(dimension_semantics=sem, vmem_limit_bytes=VMEM_LIMIT)


def _dot(a, b):
    return jnp.dot(a.astype(BF16), b.astype(BF16), preferred_element_type=F32)


def _split3(x):
    hi = x.astype(BF16)
    r1 = x - hi.astype(F32)
    mid = r1.astype(BF16)
    lo = (r1 - mid.astype(F32)).astype(BF16)
    return hi, mid, lo


def _sigmoid(x):
    return 1.0 / (1.0 + jnp.exp(-x))


def _softplus(x):
    return jnp.maximum(x, 0.0) + jnp.log(1.0 + jnp.exp(-jnp.abs(x)))


def _adaln_kernel(c_ref, w_ref, b_ref, o_ref):
    c = c_ref[...]
    s = c * _sigmoid(c)
    o_ref[0] = _dot(s, w_ref[0]) + b_ref[0]


def adaln_all(c_pad, w, b, tn=1024):
    nl, d, n = w.shape
    rows = c_pad.shape[0]
    return pl.pallas_call(
        _adaln_kernel,
        grid=(nl, n // tn),
        in_specs=[pl.BlockSpec((rows, d), lambda l, j: (0, 0)),
                  pl.BlockSpec((1, d, tn), lambda l, j: (l, 0, j)),
                  pl.BlockSpec((1, 1, tn), lambda l, j: (l, 0, j))],
        out_specs=pl.BlockSpec((1, rows, tn), lambda l, j: (l, 0, j)),
        out_shape=jax.ShapeDtypeStruct((nl, rows, n), F32),
        compiler_params=_cparams(("arbitrary", "arbitrary")),
        name="adaln",
    )(c_pad, w, b.reshape(nl, 1, n))


def _rope_tile(acc, cosf, sins, qscale):
    pieces = []
    for g in range(acc.shape[1] // LANES):
        xg = acc[:, g * LANES:(g + 1) * LANES]
        rot = pltpu.roll(xg, LANES // 2, axis=1)
        pieces.append((xg * cosf + rot * sins) * qscale)
    return jnp.concatenate(pieces, axis=1) if len(pieces) > 1 else pieces[0]


def _mm_kernel(*refs, has_norm, mode, emit_h, rope_tiles, qscale, tail_hw):
    it = iter(refs)
    x_ref = next(it)
    if has_norm:
        g_ref, sh_ref, sc_ref = next(it), next(it), next(it)
    w_ref = next(it)
    if mode == "rope":
        cos_ref, sin_ref = next(it), next(it)
    if mode == "res":
        res_ref, gate_ref = next(it), next(it)
    o_ref = next(it)
    if emit_h:
        hout_ref = next(it)
    if tail_hw:
        tail_ref = next(it)
    if has_norm:
        h_scr = next(it)
    j = pl.program_id(1)

    if has_norm:
        @pl.when(j == 0)
        def _():
            x = x_ref[...]
            ms = jnp.mean(x * x, axis=-1, keepdims=True)
            y = x * lax.rsqrt(ms + NORM_EPS) * g_ref[...]
            h = (y * (1.0 + sc_ref[0]) + sh_ref[0]).astype(BF16)
            h_scr[...] = h
            if emit_h:
                hout_ref[...] = h
        lhs = h_scr[...]
    else:
        lhs = x_ref[...]

    acc = jnp.dot(lhs, w_ref[...], preferred_element_type=F32)
    if mode == "rope":
        @pl.when(j < rope_tiles)
        def _():
            o_ref[...] = _rope_tile(acc, cos_ref[...], sin_ref[...], qscale).astype(o_ref.dtype)

        if tail_hw:
            @pl.when(j >= rope_tiles)
            def _():
                tn, tm = acc.shape[1], acc.shape[0]
                tail_ref[0, :, 0] = acc.T.reshape(tn // tail_hw, tail_hw, tm).astype(tail_ref.dtype)
    elif mode == "res":
        o_ref[...] = (res_ref[...] + gate_ref[0] * acc).astype(o_ref.dtype)
    else:
        o_ref[...] = acc.astype(o_ref.dtype)


def fused_matmul(x, w, *, seq, norm=None, rope=None, res=None, out_dtype=F32, emit_h=False,
                 tail_hw=0, tm=1024, tn=512):
    t, k = x.shape
    n = w.shape[1]
    tn = min(tn, n)
    assert t % tm == 0 and n % tn == 0 and seq % tm == 0
    bidx = lambda i: (i * tm) // seq
    in_specs = [pl.BlockSpec((tm, k), lambda i, j: (i, 0))]
    args = [x]
    if norm is not None:
        g, sh, sc = norm
        in_specs += [pl.BlockSpec((1, k), lambda i, j: (0, 0)),
                     pl.BlockSpec((1, 1, k), lambda i, j: (bidx(i), 0, 0)),
                     pl.BlockSpec((1, 1, k), lambda i, j: (bidx(i), 0, 0))]
        args += [g, sh, sc]
    in_specs.append(pl.BlockSpec((k, tn), lambda i, j: (0, j)))
    args.append(w)
    mode, rope_tiles, qscale = "none", 0, 1.0
    if rope is not None:
        cosf, sins, rope_cols, qscale = rope
        mode, rope_tiles = "rope", rope_cols // tn
        in_specs += [pl.BlockSpec((tm, LANES), lambda i, j: (i, 0)),
                     pl.BlockSpec((tm, LANES), lambda i, j: (i, 0))]
        args += [cosf, sins]
    if res is not None:
        r, gate = res
        mode = "res"
        in_specs += [pl.BlockSpec((tm, tn), lambda i, j: (i, j)),
                     pl.BlockSpec((1, 1, tn), lambda i, j: (bidx(i), 0, j))]
        args += [r, gate]
    out_specs = [pl.BlockSpec((tm, tn), lambda i, j: (i, j))]
    out_shape = [jax.ShapeDtypeStruct((t, n), out_dtype)]
    if emit_h:
        out_specs.append(pl.BlockSpec((tm, k), lambda i, j: (i, 0)))
        out_shape.append(jax.ShapeDtypeStruct((t, k), BF16))
    if mode == "rope" and rope_cols < n:
        assert tail_hw and tn % tail_hw == 0 and (n - rope_cols) % tn == 0
        rpb = seq // tm
        out_specs[0] = pl.BlockSpec((tm, tn), lambda i, j: (i, jnp.minimum(j, rope_tiles - 1)))
        out_shape[0] = jax.ShapeDtypeStruct((t, rope_cols), out_dtype)
        out_specs.append(pl.BlockSpec(
            (1, tn // tail_hw, 1, tail_hw, tm),
            lambda i, j: (bidx(i), jnp.maximum(j - rope_tiles, 0), i - bidx(i) * rpb, 0, 0)))
        out_shape.append(jax.ShapeDtypeStruct(
            (t // seq, (n - rope_cols) // tail_hw, rpb, tail_hw, tm), out_dtype))
    else:
        tail_hw = 0
    scratch = [pltpu.VMEM((tm, k), BF16)] if norm is not None else []
    outs = pl.pallas_call(
        functools.partial(_mm_kernel, has_norm=norm is not None, mode=mode, emit_h=emit_h,
                          rope_tiles=rope_tiles, qscale=qscale, tail_hw=tail_hw),
        grid=(t // tm, n // tn),
        in_specs=in_specs, out_specs=out_specs, out_shape=out_shape,
        scratch_shapes=scratch,
        compiler_params=_cparams(("arbitrary", "arbitrary")),
        name="fused_mm_" + mode,
    )(*args)
    return outs if len(outs) > 1 else outs[0]


GDN_ROWS = 256
GDN_W = GDN_HEADS * GDN_HEAD_DIM


def _gdn_kernel(qkv_ref, z_ref, a_ref, b_ref, at_ref, convw_ref, alog_ref, dtb_ref, alogc_ref,
                dtbc_ref, ogain_ref, o_ref, xbuf, qkvs, qn_s, kn_s, kb_s, qg_s, kd_s, kbgv_s, dec_s,
                l_s, a_s, tinv_s, tinvb_s, t1_s, w_s, u_s, unew_s, state, stateb):
    r = pl.program_id(1)
    rows = GDN_ROWS
    hd = GDN_HEAD_DIM
    nh = GDN_HEADS
    ncs = rows // CHUNK
    units = [(ci, h) for ci in range(ncs) for h in range(nh)]
    rsl = lambda ci: slice(ci * CHUNK, (ci + 1) * CHUNK)
    csl = lambda h: slice(h * hd, (h + 1) * hd)

    @pl.when(r == 0)
    def _():
        state[...] = jnp.zeros_like(state)
        stateb[...] = jnp.zeros_like(stateb)
        xbuf[0:SUBLANES, :] = jnp.zeros((SUBLANES, 3 * GDN_W), F32)

    cb = 512
    for c in range(3 * GDN_W // cb):
        cs = slice(c * cb, (c + 1) * cb)
        xbuf[SUBLANES:SUBLANES + rows, cs] = qkv_ref[:, cs].astype(F32)
        acc = convw_ref[3:4, cs] * xbuf[SUBLANES:SUBLANES + rows, cs]
        for i in range(CONV_K - 1):
            off = SUBLANES - (CONV_K - 1) + i
            acc = acc + convw_ref[i:i + 1, cs] * xbuf[off:off + rows, cs]
        qkvs[:, cs] = acc * _sigmoid(acc)
        xbuf[0:SUBLANES, cs] = xbuf[rows:rows + SUBLANES, cs]

    neg_a = -jnp.exp(alog_ref[...])
    g_all = neg_a * _softplus(a_ref[...] + dtb_ref[...])
    beta_all = _sigmoid(b_ref[...])
    neg_ac = -jnp.exp(alogc_ref[...])

    ii = lax.broadcasted_iota(jnp.int32, (CHUNK, CHUNK), 0)
    jj = lax.broadcasted_iota(jnp.int32, (CHUNK, CHUNK), 1)
    strict = ii > jj
    eye = ii == jj
    tril = jnp.where(ii >= jj, 1.0, 0.0).astype(BF16)
    triu = jnp.where(ii <= jj, 1.0, 0.0).astype(BF16)
    eye_f = jnp.where(eye, 1.0, 0.0).astype(F32)
    sib_masks = []
    for m in range(CHUNK.bit_length() - 1):
        sib_masks.append(((ii >> (m + 1)) == (jj >> (m + 1))) & ((ii >> m) != (jj >> m)) & strict)
    ogain = ogain_ref[...]

    glast = []
    for ci in range(ncs):
        rs = rsl(ci)
        g = g_all[rs]
        beta = beta_all[rs]
        gt = neg_ac * _softplus(at_ref[0, ci] + dtbc_ref[...])
        G = sum(jnp.dot(tril, p, preferred_element_type=F32) for p in _split3(g))
        GT = sum(jnp.dot(p, triu, preferred_element_type=F32) for p in _split3(gt))
        eG = jnp.exp(G)
        eGl = jnp.exp(G[CHUNK - 1:CHUNK, :] - G)
        glast.append(jnp.exp(G[CHUNK - 1:CHUNK, :]))
        for h in range(nh):
            cs = csl(h)
            q = qkvs[rs, cs]
            k = qkvs[rs, GDN_W + h * hd:GDN_W + (h + 1) * hd]
            v = qkvs[rs, 2 * GDN_W + h * hd:2 * GDN_W + (h + 1) * hd]
            q = q * lax.rsqrt(jnp.sum(q * q, axis=-1, keepdims=True) + NORM_EPS) * (hd ** -0.5)
            k = k * lax.rsqrt(jnp.sum(k * k, axis=-1, keepdims=True) + NORM_EPS)
            beta_c = beta[:, h:h + 1]
            eG_c = eG[:, h:h + 1]
            kb = k * beta_c
            qn_s[rs, cs] = q.astype(BF16)
            kn_s[rs, cs] = k.astype(BF16)
            kb_s[rs, cs] = kb.astype(BF16)
            qg_s[rs, cs] = (q * eG_c).astype(BF16)
            kd_s[rs, cs] = (k * eGl[:, h:h + 1]).astype(BF16)
            kbgv_s[rs, 2 * h * hd:(2 * h + 1) * hd] = (kb * eG_c).astype(BF16)
            kbgv_s[rs, (2 * h + 1) * hd:(2 * h + 2) * hd] = (v * beta_c).astype(BF16)
            diff = G[:, h:h + 1] - GT[h:h + 1, :]
            dec_s[ci * nh + h] = jnp.exp(jnp.where(strict, diff, -jnp.inf))

    nt = lambda a, b: lax.dot_general(a, b, (((1,), (1,)), ((), ())), preferred_element_type=F32)
    mm = lambda a, b: jnp.dot(a, b, preferred_element_type=F32)

    for u, (ci, h) in enumerate(units):
        rs, cs = rsl(ci), csl(h)
        dec = dec_s[u]
        kn = kn_s[rs, cs]
        lmat = nt(kb_s[rs, cs], kn) * dec
        l_s[u] = lmat
        t0 = eye_f - jnp.where(sib_masks[0], lmat, 0.0)
        tinv_s[u] = t0
        tinvb_s[u] = t0.astype(BF16)
        a_s[u] = (nt(qn_s[rs, cs], kn) * jnp.where(eye, 1.0, dec)).astype(BF16)

    for sm in sib_masks[1:]:
        for u in range(len(units)):
            t1_s[u] = mm(jnp.where(sm, l_s[u], 0.0).astype(BF16), tinvb_s[u]).astype(BF16)
        for u in range(len(units)):
            tn = tinv_s[u] - mm(tinvb_s[u], t1_s[u])
            tinv_s[u] = tn
            tinvb_s[u] = tn.astype(BF16)

    for u, (ci, h) in enumerate(units):
        rs, cs = rsl(ci), csl(h)
        wu = mm(tinvb_s[u], kbgv_s[rs, 2 * h * hd:(2 * h + 2) * hd])
        w_s[rs, cs] = wu[:, :hd].astype(BF16)
        u_s[rs, cs] = wu[:, hd:]

    for ci in range(ncs):
        rs = rsl(ci)
        for h in range(nh):
            cs = csl(h)
            unew_s[h] = (u_s[rs, cs] - mm(w_s[rs, cs], stateb[h])).astype(BF16)
        for h in range(nh):
            cs = csl(h)
            un = unew_s[h]
            o = mm(qg_s[rs, cs], stateb[h]) + mm(a_s[ci * nh + h], un)
            snew = glast[ci][:, h:h + 1] * state[h] + lax.dot_general(
                kd_s[rs, cs], un, (((0,), (0,)), ((), ())), preferred_element_type=F32)
            state[h] = snew
            stateb[h] = snew.astype(BF16)
            o = o * lax.rsqrt(jnp.mean(o * o, axis=-1, keepdims=True) + NORM_EPS) * ogain
            zh = z_ref[rs, cs].astype(F32)
            o_ref[rs, cs] = (o * (zh * _sigmoid(zh))).astype(o_ref.dtype)


def gdn_core(proj, a, b, conv_w, a_log, dt_bias, o_gain, *, batch, seq):
    t = proj.shape[0]
    rows = GDN_ROWS
    nr = seq // rows
    cpr = rows // CHUNK
    h = GDN_HEADS
    a_t = a.reshape(batch, seq // CHUNK, CHUNK, h).transpose(0, 1, 3, 2)
    a_t = a_t.reshape(batch * nr, cpr, h, CHUNK)
    row_map = lambda bi, r: (bi * nr + r, 0)
    full = lambda bi, r: (0, 0)
    return pl.pallas_call(
        _gdn_kernel,
        grid=(batch, nr),
        in_specs=[pl.BlockSpec((rows, 3 * GDN_W), row_map),
                  pl.BlockSpec((rows, GDN_W), lambda bi, r: (bi * nr + r, 3)),
                  pl.BlockSpec((rows, h), row_map),
                  pl.BlockSpec((rows, h), row_map),
                  pl.BlockSpec((1, cpr, h, CHUNK), lambda bi, r: (bi * nr + r, 0, 0, 0)),
                  pl.BlockSpec((CONV_K, 3 * GDN_W), full),
                  pl.BlockSpec((1, h), full), pl.BlockSpec((1, h), full),
                  pl.BlockSpec((h, 1), full), pl.BlockSpec((h, 1), full),
                  pl.BlockSpec((1, GDN_HEAD_DIM), full)],
        out_specs=pl.BlockSpec((rows, GDN_W), row_map),
        out_shape=jax.ShapeDtypeStruct((t, GDN_W), BF16),
        scratch_shapes=[pltpu.VMEM((SUBLANES + rows, 3 * GDN_W), F32),
                        pltpu.VMEM((rows, 3 * GDN_W), F32),
                        pltpu.VMEM((rows, GDN_W), BF16),
                        pltpu.VMEM((rows, GDN_W), BF16),
                        pltpu.VMEM((rows, GDN_W), BF16),
                        pltpu.VMEM((rows, GDN_W), BF16),
                        pltpu.VMEM((rows, GDN_W), BF16),
                        pltpu.VMEM((rows, 2 * GDN_W), BF16),
                        pltpu.VMEM((cpr * h, CHUNK, CHUNK), F32),
                        pltpu.VMEM((cpr * h, CHUNK, CHUNK), F32),
                        pltpu.VMEM((cpr * h, CHUNK, CHUNK), BF16),
                        pltpu.VMEM((cpr * h, CHUNK, CHUNK), F32),
                        pltpu.VMEM((cpr * h, CHUNK, CHUNK), BF16),
                        pltpu.VMEM((cpr * h, CHUNK, CHUNK), BF16),
                        pltpu.VMEM((rows, GDN_W), BF16),
                        pltpu.VMEM((rows, GDN_W), F32),
                        pltpu.VMEM((h, CHUNK, GDN_HEAD_DIM), BF16),
                        pltpu.VMEM((h, GDN_HEAD_DIM, GDN_HEAD_DIM), F32),
                        pltpu.VMEM((h, GDN_HEAD_DIM, GDN_HEAD_DIM), BF16)],
        compiler_params=_cparams(("arbitrary", "arbitrary")),
        name="gdn_core",
    )(proj, proj, a, b, a_t, conv_w, a_log.reshape(1, h), dt_bias.reshape(1, h),
      a_log.reshape(h, 1), dt_bias.reshape(h, 1), o_gain.reshape(1, GDN_HEAD_DIM))


ATT_TQ = 1024
ATT_TK = 1024


def _attn_kernel(lam_ref, q_ref, k_ref, vt_ref, g_ref, o_ref, m_scr, l_scr, acc_scr, *, out_scale):
    i = pl.program_id(2)
    d = DIFF_HEAD_DIM
    tq, tk = ATT_TQ, ATT_TK
    q = q_ref[...]
    qs = [q[:, c * d:(c + 1) * d] for c in range(2)]
    m_scr[...] = jnp.full(m_scr.shape, -jnp.inf, F32)
    l_scr[...] = jnp.zeros(l_scr.shape, F32)
    acc_scr[...] = jnp.zeros(acc_scr.shape, F32)

    def tile(j, masked):
        k = k_ref[pl.ds(pl.multiple_of(j * tk, tk), tk), :]
        vt = vt_ref[0, 0, j]
        sts = [lax.dot_general(k[:, c * d:(c + 1) * d], qs[c], (((1,), (1,)), ((), ())),
                               preferred_element_type=F32) for c in range(2)]
        if masked:
            kpos = lax.broadcasted_iota(jnp.int32, (tk, tq), 0)
            qpos = lax.broadcasted_iota(jnp.int32, (tk, tq), 1)
            sts = [jnp.where(kpos <= qpos, st, -jnp.inf) for st in sts]
        ps, alphas = [], []
        for c in range(2):
            m_prev = m_scr[c]
            m_new = jnp.maximum(m_prev, jnp.max(sts[c], axis=0, keepdims=True))
            alpha = jnp.exp2(m_prev - m_new)
            p = jnp.exp2(sts[c] - m_new)
            l_scr[c] = alpha * l_scr[c] + jnp.sum(p, axis=0, keepdims=True)
            m_scr[c] = m_new
            ps.append(p.astype(BF16))
            alphas.append(alpha)
        for c in range(2):
            acc_scr[c] = alphas[c] * acc_scr[c] + jnp.dot(vt, ps[c],
                                                          preferred_element_type=F32)

    def body(j, carry):
        tile(j, False)
        return carry

    lax.fori_loop(0, i, body, 0)
    tile(i, True)

    lam = lam_ref[0]
    o = acc_scr[0] / l_scr[0] - lam * (acc_scr[1] / l_scr[1])
    o = o * lax.rsqrt(jnp.mean(o * o, axis=0, keepdims=True) + NORM_EPS) * (g_ref[...] * out_scale)
    o_ref[...] = o.T.astype(o_ref.dtype)


def diff_attn_core(q, k, vt, lam, subln_g, *, batch, seq, out_scale):
    t, width = q.shape
    hw = 2 * DIFF_HEAD_DIM
    assert ATT_TQ == ATT_TK
    nq, nk = seq // ATT_TQ, seq // ATT_TK
    assert vt.shape == (batch, DIFF_HEADS, nk, hw, ATT_TK)
    return pl.pallas_call(
        functools.partial(_attn_kernel, out_scale=out_scale),
        grid=(batch, DIFF_HEADS, nq),
        in_specs=[pl.BlockSpec(memory_space=pltpu.SMEM),
                  pl.BlockSpec((ATT_TQ, hw), lambda b, h, i: (b * nq + i, h)),
                  pl.BlockSpec((seq, hw), lambda b, h, i: (b, h)),
                  pl.BlockSpec((1, 1, nk, hw, ATT_TK), lambda b, h, i: (b, h, 0, 0, 0)),
                  pl.BlockSpec((hw, 1), lambda b, h, i: (0, 0))],
        out_specs=pl.BlockSpec((ATT_TQ, hw), lambda b, h, i: (b * nq + i, h)),
        out_shape=jax.ShapeDtypeStruct((t, width), BF16),
        scratch_shapes=[pltpu.VMEM((2, 1, ATT_TQ), F32), pltpu.VMEM((2, 1, ATT_TQ), F32),
                        pltpu.VMEM((2, hw, ATT_TQ), F32)],
        compiler_params=_cparams(("arbitrary", "arbitrary", "arbitrary")),
        name="diff_attn",
    )(lam, q, k, vt, subln_g.reshape(hw, 1))


PEER_TT = 512
PEER_TE = 1024
_CAND_COUNTS = tuple(min(PEER_TOPK, (PEER_TOPK + 1) // (i + 1)) for i in range(PEER_TOPK))
_N_CAND = sum(_CAND_COUNTS)
_N_CAND_PAD = -(-_N_CAND // SUBLANES) * SUBLANES


def _extract_top(x, n):
    rows = x.shape[0]
    iota = lax.broadcasted_iota(jnp.int32, x.shape, 0)
    vals = []
    for _ in range(n):
        m = jnp.max(x, axis=0, keepdims=True)
        idx = jnp.min(jnp.where(x == m, iota, rows), axis=0, keepdims=True)
        x = jnp.where(iota == idx, -jnp.inf, x)
        vals.append(m)
    return vals


def _oddeven_merge_sort_pairs(n):
    pairs = []
    p = 1
    while p < n:
        k = p
        while k >= 1:
            for j in range(k % p, n - k, 2 * k):
                for i in range(min(k, n - j - k)):
                    if (i + j) // (2 * p) == (i + j + k) // (2 * p):
                        pairs.append((i + j, i + j + k))
            k //= 2
        p *= 2
    return pairs


_SORT16 = tuple(_oddeven_merge_sort_pairs(PEER_TOPK))


def _top16_sorted(x):
    n = PEER_TOPK
    vs = [x[SUBLANES * r:SUBLANES * (r + 1), :] for r in range(n)]
    for i, j in _SORT16:
        vs[i], vs[j] = jnp.maximum(vs[i], vs[j]), jnp.minimum(vs[i], vs[j])
    shift = SUBLANES // 2
    while shift >= 1:
        vs = [jnp.maximum(vs[i], pltpu.roll(vs[n - 1 - i], shift, axis=0)) for i in range(n)]
        k = n // 2
        while k >= 1:
            for i in range(n):
                if i & k == 0:
                    vs[i], vs[i + k] = jnp.maximum(vs[i], vs[i + k]), jnp.minimum(vs[i], vs[i + k])
            k //= 2
        shift //= 2
    return vs


def _peer_topk_kernel(q_ref, keys_ref, s1_ref, s2_ref, tau_ref):
    tt = q_ref.shape[0]
    kparts = [_split3(keys_ref[j])[:2] for j in range(2)]
    for p in range(PEER_HEADS):
        sc = []
        for j in range(2):
            c0 = (2 * p + j) * PEER_HALF
            qh, ql = _split3(q_ref[:, c0:c0 + PEER_HALF])[:2]
            kh, kl = kparts[j]
            nt = lambda a, b: lax.dot_general(a, b, (((1,), (1,)), ((), ())),
                                              preferred_element_type=F32)
            sc.append(nt(kh, qh) + (nt(kh, ql) + nt(kl, qh)))
        v1 = [v[0:1, :] for v in _top16_sorted(sc[0])]
        v2 = jnp.concatenate([v[0:1, :] for v in _top16_sorted(sc[1])], axis=0)
        cand = [v1[i] + v2[0:_CAND_COUNTS[i]] for i in range(PEER_TOPK)]
        if _N_CAND_PAD > _N_CAND:
            cand.append(jnp.full((_N_CAND_PAD - _N_CAND, tt), -jnp.inf, F32))
        best = _extract_top(jnp.concatenate(cand, axis=0), PEER_TOPK + 1)
        zsum = sum(jnp.exp(bk - best[0]) for bk in best[:PEER_TOPK])
        mz = best[0] + jnp.log(zsum)
        tau = 0.5 * (best[PEER_TOPK - 1] + best[PEER_TOPK])
        s1_ref[0, p] = (sc[0] - mz) * LOG2E
        s2_ref[0, p] = sc[1] * LOG2E
        tau_ref[0, p:p + 1, :] = (tau - mz) * LOG2E


def peer_topk(q, sub_keys):
    t = q.shape[0]
    tt = PEER_TT
    nt = t // tt
    big = jax.ShapeDtypeStruct((nt, PEER_HEADS, N_KEYS, tt), F32)
    return pl.pallas_call(
        _peer_topk_kernel,
        grid=(nt,),
        in_specs=[pl.BlockSpec((tt, q.shape[1]), lambda i: (i, 0)),
                  pl.BlockSpec((2, N_KEYS, PEER_HALF), lambda i: (0, 0, 0))],
        out_specs=[pl.BlockSpec((1, PEER_HEADS, N_KEYS, tt), lambda i: (i, 0, 0, 0)),
                   pl.BlockSpec((1, PEER_HEADS, N_KEYS, tt), lambda i: (i, 0, 0, 0)),
                   pl.BlockSpec((1, PEER_HEADS, tt), lambda i: (i, 0, 0))],
        out_shape=[big, big, jax.ShapeDtypeStruct((nt, PEER_HEADS, tt), F32)],
        compiler_params=_cparams(("arbitrary",)),
        name="peer_topk",
    )(q, sub_keys)


def _peer_expert_kernel(h_ref, s1_ref, s2_ref, tau_ref, u_ref, vt_ref, x_ref, gate_ref, fin_ref,
                        o_ref, acc, act_scr, w_scr, *, final_norm):
    e = pl.program_id(1)
    ne = pl.num_programs(1)
    tt = h_ref.shape[0]
    a_per = PEER_TE // N_KEYS
    gb = 32

    @pl.when(e == 0)
    def _():
        acc[...] = jnp.zeros_like(acc)

    act_scr[...] = lax.dot_general(u_ref[0], h_ref[...], (((1,), (1,)), ((), ())),
                                   preferred_element_type=F32)
    for ai in range(a_per):
        a = e * a_per + ai
        for tc in range(tt // LANES):
            cols = slice(tc * LANES, (tc + 1) * LANES)
            for bi in range(N_KEYS // gb):
                bs = slice(bi * gb, (bi + 1) * gb)
                rs = slice(ai * N_KEYS + bi * gb, ai * N_KEYS + (bi + 1) * gb)
                gsum = jnp.zeros((gb, LANES), F32)
                for p in range(PEER_HEADS):
                    s1_row = s1_ref[0, p, pl.ds(a, 1), :][:, cols]
                    tsum = s1_row + s2_ref[0, p, bs, cols]
                    gsum = gsum + jnp.where(tsum > tau_ref[0, p:p + 1, cols], jnp.exp2(tsum), 0.0)
                act = act_scr[rs, cols]
                gel = 0.5 * act * (1.0 + lax.erf(act * (2.0 ** -0.5)))
                w_scr[rs, cols] = (gsum * gel).astype(BF16)
    acc[...] += jnp.dot(vt_ref[0], w_scr[...], preferred_element_type=F32)

    @pl.when(e == ne - 1)
    def _():
        o = x_ref[...] + gate_ref[0] * acc[...].T
        if final_norm:
            o = o * lax.rsqrt(jnp.mean(o * o, axis=-1, keepdims=True) + NORM_EPS) * fin_ref[...]
        o_ref[...] = o


def peer_experts(h, s1, s2, tau, u_bf, vt_bf, x, gate, final_g, *, layer, seq, final_norm):
    t, d = h.shape
    n_exp = u_bf.shape[1]
    tt, te = PEER_TT, PEER_TE
    bidx = lambda i: (i * tt) // seq
    once = pl.Buffered(1)
    return pl.pallas_call(
        functools.partial(_peer_expert_kernel, final_norm=final_norm),
        grid=(t // tt, n_exp // te),
        in_specs=[pl.BlockSpec((tt, d), lambda i, e: (i, 0), pipeline_mode=once),
                  pl.BlockSpec((1, PEER_HEADS, N_KEYS, tt), lambda i, e: (i, 0, 0, 0),
                               pipeline_mode=once),
                  pl.BlockSpec((1, PEER_HEADS, N_KEYS, tt), lambda i, e: (i, 0, 0, 0),
                               pipeline_mode=once),
                  pl.BlockSpec((1, PEER_HEADS, tt), lambda i, e: (i, 0, 0)),
                  pl.BlockSpec((1, te, d), lambda i, e: (layer, e, 0)),
                  pl.BlockSpec((1, d, te), lambda i, e: (layer, 0, e)),
                  pl.BlockSpec((tt, d), lambda i, e: (i, 0), pipeline_mode=once),
                  pl.BlockSpec((1, 1, d), lambda i, e: (bidx(i), 0, 0)),
                  pl.BlockSpec((1, d), lambda i, e: (0, 0))],
        out_specs=pl.BlockSpec((tt, d), lambda i, e: (i, 0)),
        out_shape=jax.ShapeDtypeStruct((t, d), F32),
        scratch_shapes=[pltpu.VMEM((d, tt), F32),
                        pltpu.VMEM((te, tt), F32), pltpu.VMEM((te, tt), BF16)],
        compiler_params=_cparams(("arbitrary", "arbitrary")),
        name="peer_experts",
    )(h, s1, s2, tau, u_bf, vt_bf, x, gate, final_g)


def _rope_tables(positions):
    d = DIFF_HEAD_DIM
    inv_freq = ROPE_THETA ** (-jnp.arange(0, d, 2, dtype=F32) / d)
    ang = positions.astype(F32)[..., None] * inv_freq
    cos, sin = jnp.cos(ang), jnp.sin(ang)
    cosf = jnp.concatenate([cos, cos], axis=-1).reshape(-1, d)
    sins = jnp.concatenate([-sin, sin], axis=-1).reshape(-1, d)
    return cosf, sins


def kernel(x, c, positions, ada_w, ada_b, norm_mix_g, norm_ffn_g, gdn_w_in, gdn_conv_w, gdn_a_log,
           gdn_dt_bias, gdn_o_gain, gdn_w_out, kv_norm_g, kv_ada_w, kv_ada_b, kv_w, diff_w_q,
           diff_lambda, diff_subln_g, diff_w_out, peer_w_q, peer_sub_keys, peer_u, peer_v, final_g):
    batch, seq, d = x.shape
    depth = ada_w.shape[0]
    assert depth > N_A_LAYERS
    t = batch * seq
    xs = x.reshape(t, d)
    cosf, sins = _rope_tables(positions)

    c_pad = jnp.pad(c, ((0, SUBLANES - batch), (0, 0)))
    mods = adaln_all(c_pad, ada_w, ada_b)[:, :batch]
    kv_mods = adaln_all(c_pad, kv_ada_w[None], kv_ada_b[None])[0, :batch]
    mod = lambda m: m[:, None, :]

    u_all = peer_u.astype(BF16)
    vt_all = jnp.swapaxes(peer_v, 1, 2).astype(BF16)

    k_sh = vt_sh = None
    for l in range(depth):
        sh1, sc1, gt1, sh2, sc2, gt2 = [mod(m) for m in jnp.split(mods[l], 6, axis=-1)]
        norm1 = (norm_mix_g[l].reshape(1, d), sh1, sc1)
        if l < N_A_LAYERS:
            w_in = gdn_w_in[l]
            w4 = 4 * GDN_W
            proj = fused_matmul(xs, w_in[:, :w4].astype(BF16), seq=seq, norm=norm1, emit_h=True,
                                out_dtype=BF16)
            proj, h1 = proj
            w_ab = jnp.pad(w_in[:, w4:], ((0, 0), (0, LANES - 2 * GDN_HEADS))).astype(BF16)
            ab = fused_matmul(h1, w_ab, seq=seq)
            o = gdn_core(proj, ab[:, :GDN_HEADS],
                         ab[:, GDN_HEADS:2 * GDN_HEADS], gdn_conv_w[l], gdn_a_log[l],
                         gdn_dt_bias[l], gdn_o_gain[l], batch=batch, seq=seq)
            xs = fused_matmul(o, gdn_w_out[l].astype(BF16), seq=seq, res=(xs, gt1))
        else:
            j = l - N_A_LAYERS
            lambda_init = 0.8 - 0.6 * math.exp(-0.3 * l)
            lp = diff_lambda[j].astype(F32)
            lam = jnp.exp(jnp.sum(lp[0] * lp[1])) - jnp.exp(jnp.sum(lp[2] * lp[3])) + lambda_init
            qw = DIFF_HEADS * 2 * DIFF_HEAD_DIM
            q = fused_matmul(xs, diff_w_q[j].astype(BF16), seq=seq, norm=norm1,
                             rope=(cosf, sins, qw, DIFF_HEAD_DIM ** -0.5 * LOG2E), out_dtype=BF16)
            o = diff_attn_core(q, k_sh, vt_sh, lam.reshape(1), diff_subln_g[j], batch=batch,
                               seq=seq, out_scale=1.0 - lambda_init)
            xs = fused_matmul(o, diff_w_out[j].astype(BF16), seq=seq, res=(xs, gt1))

        norm2 = (norm_ffn_g[l].reshape(1, d), sh2, sc2)
        pq, h2 = fused_matmul(xs, peer_w_q[l].astype(BF16), seq=seq, norm=norm2, emit_h=True)
        s1, s2, tau = peer_topk(pq, peer_sub_keys[l])
        xs = peer_experts(h2, s1, s2, tau, u_all, vt_all, xs, gt2, final_g.reshape(1, d),
                          layer=l, seq=seq, final_norm=(l == depth - 1))

        if l == N_A_LAYERS - 1:
            kvsh, kvsc = [mod(m) for m in jnp.split(kv_mods, 2, axis=-1)]
            qw = DIFF_HEADS * 2 * DIFF_HEAD_DIM
            k_sh, vt_sh = fused_matmul(xs, kv_w.astype(BF16), seq=seq,
                                       norm=(kv_norm_g.reshape(1, d), kvsh, kvsc),
                                       rope=(cosf, sins, qw, 1.0), out_dtype=BF16,
                                       tail_hw=2 * DIFF_HEAD_DIM, tm=ATT_TK)

    return xs.reshape(batch, seq, d)
```

```python
import functools
import math

import jax
import jax.numpy as jnp
from jax import lax
from jax.experimental import pallas as pl
from jax.experimental.pallas import tpu as pltpu

F32 = jnp.float32
BF16 = jnp.bfloat16

NORM_EPS = 1e-6
N_A_LAYERS = 2
GDN_HEADS = 16
GDN_HEAD_DIM = 128
CONV_K = 4
CHUNK = 64
DIFF_HEADS = 8
DIFF_HEAD_DIM = 128
ROPE_THETA = 10000.0
PEER_HEADS = 8
PEER_HALF = 128
N_KEYS = 128
PEER_TOPK = 16

LOG2E = math.log2(math.e)

LANES = 128
SUBLANES = 8
VMEM_LIMIT = 56 * 1024 * 1024


def _cparams(sem):
    return pltpu.CompilerParams(dimension_semantics=sem, vmem_limit_bytes=VMEM_LIMIT)


def _dot(a, b):
    return jnp.dot(a.astype(BF16), b.astype(BF16), preferred_element_type=F32)


def _split3(x):
    hi = x.astype(BF16)
    r1 = x - hi.astype(F32)
    mid = r1.astype(BF16)
    lo = (r1 - mid.astype(F32)).astype(BF16)
    return hi, mid, lo


def _sigmoid(x):
    return 1.0 / (1.0 + jnp.exp(-x))


def _softplus(x):
    return jnp.maximum(x, 0.0) + jnp.log(1.0 + jnp.exp(-jnp.abs(x)))


def _adaln_kernel(c_ref, w_ref, b_ref, o_ref):
    c = c_ref[...]
    s = c * _sigmoid(c)
    o_ref[0] = _dot(s, w_ref[0]) + b_ref[0]


def adaln_all(c_pad, w, b, tn=1024):
    nl, d, n = w.shape
    rows = c_pad.shape[0]
    return pl.pallas_call(
        _adaln_kernel,
        grid=(nl, n // tn),
        in_specs=[pl.BlockSpec((rows, d), lambda l, j: (0, 0)),
                  pl.BlockSpec((1, d, tn), lambda l, j: (l, 0, j)),
                  pl.BlockSpec((1, 1, tn), lambda l, j: (l, 0, j))],
        out_specs=pl.BlockSpec((1, rows, tn), lambda l, j: (l, 0, j)),
        out_shape=jax.ShapeDtypeStruct((nl, rows, n), F32),
        compiler_params=_cparams(("arbitrary", "arbitrary")),
        name="adaln",
    )(c_pad, w, b.reshape(nl, 1, n))


def _rope_tile(acc, cosf, sins, qscale):
    pieces = []
    for g in range(acc.shape[1] // LANES):
        xg = acc[:, g * LANES:(g + 1) * LANES]
        rot = pltpu.roll(xg, LANES // 2, axis=1)
        pieces.append((xg * cosf + rot * sins) * qscale)
    return jnp.concatenate(pieces, axis=1) if len(pieces) > 1 else pieces[0]


def _mm_kernel(*refs, has_norm, mode, emit_h, rope_tiles, qscale, tail_hw):
    it = iter(refs)
    x_ref = next(it)
    if has_norm:
        g_ref, sh_ref, sc_ref = next(it), next(it), next(it)
    w_ref = next(it)
    if mode == "rope":
        cos_ref, sin_ref = next(it), next(it)
    if mode == "res":
        res_ref, gate_ref = next(it), next(it)
    o_ref = next(it)
    if emit_h:
        hout_ref = next(it)
    if tail_hw:
        tail_ref = next(it)
    if has_norm:
        h_scr = next(it)
    j = pl.program_id(1)

    if has_norm:
        @pl.when(j == 0)
        def _():
            x = x_ref[...]
            ms = jnp.mean(x * x, axis=-1, keepdims=True)
            y = x * lax.rsqrt(ms + NORM_EPS) * g_ref[...]
            h = (y * (1.0 + sc_ref[0]) + sh_ref[0]).astype(BF16)
            h_scr[...] = h
            if emit_h:
                hout_ref[...] = h
        lhs = h_scr[...]
    else:
        lhs = x_ref[...]

    acc = jnp.dot(lhs, w_ref[...], preferred_element_type=F32)
    if mode == "rope":
        @pl.when(j < rope_tiles)
        def _():
            o_ref[...] = _rope_tile(acc, cos_ref[...], sin_ref[...], qscale).astype(o_ref.dtype)

        if tail_hw:
            @pl.when(j >= rope_tiles)
            def _():
                tn, tm = acc.shape[1], acc.shape[0]
                tail_ref[0, :, 0] = acc.T.reshape(tn // tail_hw, tail_hw, tm).astype(tail_ref.dtype)
    elif mode == "res":
        o_ref[...] = (res_ref[...] + gate_ref[0] * acc).astype(o_ref.dtype)
    else:
        o_ref[...] = acc.astype(o_ref.dtype)


def fused_matmul(x, w, *, seq, norm=None, rope=None, res=None, out_dtype=F32, emit_h=False,
                 tail_hw=0, tm=1024, tn=512):
    t, k = x.shape
    n = w.shape[1]
    tn = min(tn, n)
    assert t % tm == 0 and n % tn == 0 and seq % tm == 0
    bidx = lambda i: (i * tm) // seq
    in_specs = [pl.BlockSpec((tm, k), lambda i, j: (i, 0))]
    args = [x]
    if norm is not None:
        g, sh, sc = norm
        in_specs += [pl.BlockSpec((1, k), lambda i, j: (0, 0)),
                     pl.BlockSpec((1, 1, k), lambda i, j: (bidx(i), 0, 0)),
                     pl.BlockSpec((1, 1, k), lambda i, j: (bidx(i), 0, 0))]
        args += [g, sh, sc]
    in_specs.append(pl.BlockSpec((k, tn), lambda i, j: (0, j)))
    args.append(w)
    mode, rope_tiles, qscale = "none", 0, 1.0
    if rope is not None:
        cosf, sins, rope_cols, qscale = rope
        mode, rope_tiles = "rope", rope_cols // tn
        in_specs += [pl.BlockSpec((tm, LANES), lambda i, j: (i, 0)),
                     pl.BlockSpec((tm, LANES), lambda i, j: (i, 0))]
        args += [cosf, sins]
    if res is not None:
        r, gate = res
        mode = "res"
        in_specs += [pl.BlockSpec((tm, tn), lambda i, j: (i, j)),
                     pl.BlockSpec((1, 1, tn), lambda i, j: (bidx(i), 0, j))]
        args += [r, gate]
    out_specs = [pl.BlockSpec((tm, tn), lambda i, j: (i, j))]
    out_shape = [jax.ShapeDtypeStruct((t, n), out_dtype)]
    if emit_h:
        out_specs.append(pl.BlockSpec((tm, k), lambda i, j: (i, 0)))
        out_shape.append(jax.ShapeDtypeStruct((t, k), BF16))
    if mode == "rope" and rope_cols < n:
        assert tail_hw and tn % tail_hw == 0 and (n - rope_cols) % tn == 0
        rpb = seq // tm
        out_specs[0] = pl.BlockSpec((tm, tn), lambda i, j: (i, jnp.minimum(j, rope_tiles - 1)))
        out_shape[0] = jax.ShapeDtypeStruct((t, rope_cols), out_dtype)
        out_specs.append(pl.BlockSpec(
            (1, tn // tail_hw, 1, tail_hw, tm),
            lambda i, j: (bidx(i), jnp.maximum(j - rope_tiles, 0), i - bidx(i) * rpb, 0, 0)))
        out_shape.append(jax.ShapeDtypeStruct(
            (t // seq, (n - rope_cols) // tail_hw, rpb, tail_hw, tm), out_dtype))
    else:
        tail_hw = 0
    scratch = [pltpu.VMEM((tm, k), BF16)] if norm is not None else []
    outs = pl.pallas_call(
        functools.partial(_mm_kernel, has_norm=norm is not None, mode=mode, emit_h=emit_h,
                          rope_tiles=rope_tiles, qscale=qscale, tail_hw=tail_hw),
        grid=(t // tm, n // tn),
        in_specs=in_specs, out_specs=out_specs, out_shape=out_shape,
        scratch_shapes=scratch,
        compiler_params=_cparams(("arbitrary", "arbitrary")),
        name="fused_mm_" + mode,
    )(*args)
    return outs if len(outs) > 1 else outs[0]


GDN_ROWS = 256
GDN_W = GDN_HEADS * GDN_HEAD_DIM


def _gdn_kernel(qkv_ref, z_ref, a_ref, b_ref, at_ref, convw_ref, alog_ref, dtb_ref, alogc_ref,
                dtbc_ref, ogain_ref, o_ref, xbuf, qkvs, qn_s, kn_s, kb_s, qg_s, kd_s, kbgv_s, dec_s,
                l_s, a_s, tinv_s, tinvb_s, t1_s, w_s, u_s, unew_s, state, stateb):
    r = pl.program_id(1)
    rows = GDN_ROWS
    hd = GDN_HEAD_DIM
    nh = GDN_HEADS
    ncs = rows // CHUNK
    units = [(ci, h) for ci in range(ncs) for h in range(nh)]
    rsl = lambda ci: slice(ci * CHUNK, (ci + 1) * CHUNK)
    csl = lambda h: slice(h * hd, (h + 1) * hd)

    @pl.when(r == 0)
    def _():
        state[...] = jnp.zeros_like(state)
        stateb[...] = jnp.zeros_like(stateb)
        xbuf[0:SUBLANES, :] = jnp.zeros((SUBLANES, 3 * GDN_W), F32)

    cb = 512
    for c in range(3 * GDN_W // cb):
        cs = slice(c * cb, (c + 1) * cb)
        xbuf[SUBLANES:SUBLANES + rows, cs] = qkv_ref[:, cs].astype(F32)
        acc = convw_ref[3:4, cs] * xbuf[SUBLANES:SUBLANES + rows, cs]
        for i in range(CONV_K - 1):
            off = SUBLANES - (CONV_K - 1) + i
            acc = acc + convw_ref[i:i + 1, cs] * xbuf[off:off + rows, cs]
        qkvs[:, cs] = acc * _sigmoid(acc)
        xbuf[0:SUBLANES, cs] = xbuf[rows:rows + SUBLANES, cs]

    neg_a = -jnp.exp(alog_ref[...])
    g_all = neg_a * _softplus(a_ref[...] + dtb_ref[...])
    beta_all = _sigmoid(b_ref[...])
    neg_ac = -jnp.exp(alogc_ref[...])

    ii = lax.broadcasted_iota(jnp.int32, (CHUNK, CHUNK), 0)
    jj = lax.broadcasted_iota(jnp.int32, (CHUNK, CHUNK), 1)
    strict = ii > jj
    eye = ii == jj
    tril = jnp.where(ii >= jj, 1.0, 0.0).astype(BF16)
    triu = jnp.where(ii <= jj, 1.0, 0.0).astype(BF16)
    eye_f = jnp.where(eye, 1.0, 0.0).astype(F32)
    sib_masks = []
    for m in range(CHUNK.bit_length() - 1):
        sib_masks.append(((ii >> (m + 1)) == (jj >> (m + 1))) & ((ii >> m) != (jj >> m)) & strict)
    ogain = ogain_ref[...]

    glast = []
    for ci in range(ncs):
        rs = rsl(ci)
        g = g_all[rs]
        beta = beta_all[rs]
        gt = neg_ac * _softplus(at_ref[0, ci] + dtbc_ref[...])
        G = sum(jnp.dot(tril, p, preferred_element_type=F32) for p in _split3(g))
        GT = sum(jnp.dot(p, triu, preferred_element_type=F32) for p in _split3(gt))
        eG = jnp.exp(G)
        eGl = jnp.exp(G[CHUNK - 1:CHUNK, :] - G)
        glast.append(jnp.exp(G[CHUNK - 1:CHUNK, :]))
        for h in range(nh):
            cs = csl(h)
            q = qkvs[rs, cs]
            k = qkvs[rs, GDN_W + h * hd:GDN_W + (h + 1) * hd]
            v = qkvs[rs, 2 * GDN_W + h * hd:2 * GDN_W + (h + 1) * hd]
            q = q * lax.rsqrt(jnp.sum(q * q, axis=-1, keepdims=True) + NORM_EPS) * (hd ** -0.5)
            k = k * lax.rsqrt(jnp.sum(k * k, axis=-1, keepdims=True) + NORM_EPS)
            beta_c = beta[:, h:h + 1]
            eG_c = eG[:, h:h + 1]
            kb = k * beta_c
            qn_s[rs, cs] = q.astype(BF16)
            kn_s[rs, cs] = k.astype(BF16)
            kb_s[rs, cs] = kb.astype(BF16)
            qg_s[rs, cs] = (q * eG_c).astype(BF16)
            kd_s[rs, cs] = (k * eGl[:, h:h + 1]).astype(BF16)
            kbgv_s[rs, 2 * h * hd:(2 * h + 1) * hd] = (kb * eG_c).astype(BF16)
            kbgv_s[rs, (2 * h + 1) * hd:(2 * h + 2) * hd] = (v * beta_c).astype(BF16)
            diff = G[:, h:h + 1] - GT[h:h + 1, :]
            dec_s[ci * nh + h] = jnp.exp(jnp.where(strict, diff, -jnp.inf))

    nt = lambda a, b: lax.dot_general(a, b, (((1,), (1,)), ((), ())), preferred_element_type=F32)
    mm = lambda a, b: jnp.dot(a, b, preferred_element_type=F32)

    for u, (ci, h) in enumerate(units):
        rs, cs = rsl(ci), csl(h)
        dec = dec_s[u]
        kn = kn_s[rs, cs]
        lmat = nt(kb_s[rs, cs], kn) * dec
        l_s[u] = lmat
        t0 = eye_f - jnp.where(sib_masks[0], lmat, 0.0)
        tinv_s[u] = t0
        tinvb_s[u] = t0.astype(BF16)
        a_s[u] = (nt(qn_s[rs, cs], kn) * jnp.where(eye, 1.0, dec)).astype(BF16)

    for sm in sib_masks[1:]:
        for u in range(len(units)):
            t1_s[u] = mm(jnp.where(sm, l_s[u], 0.0).astype(BF16), tinvb_s[u]).astype(BF16)
        for u in range(len(units)):
            tn = tinv_s[u] - mm(tinvb_s[u], t1_s[u])
            tinv_s[u] = tn
            tinvb_s[u] = tn.astype(BF16)

    for u, (ci, h) in enumerate(units):
        rs, cs = rsl(ci), csl(h)
        wu = mm(tinvb_s[u], kbgv_s[rs, 2 * h * hd:(2 * h + 2) * hd])
        w_s[rs, cs] = wu[:, :hd].astype(BF16)
        u_s[rs, cs] = wu[:, hd:]

    for ci in range(ncs):
        rs = rsl(ci)
        for h in range(nh):
            cs = csl(h)
            unew_s[h] = (u_s[rs, cs] - mm(w_s[rs, cs], stateb[h])).astype(BF16)
        for h in range(nh):
            cs = csl(h)
            un = unew_s[h]
            o = mm(qg_s[rs, cs], stateb[h]) + mm(a_s[ci * nh + h], un)
            snew = glast[ci][:, h:h + 1] * state[h] + lax.dot_general(
                kd_s[rs, cs], un, (((0,), (0,)), ((), ())), preferred_element_type=F32)
            state[h] = snew
            stateb[h] = snew.astype(BF16)
            o = o * lax.rsqrt(jnp.mean(o * o, axis=-1, keepdims=True) + NORM_EPS) * ogain
            zh = z_ref[rs, cs].astype(F32)
            o_ref[rs, cs] = (o * (zh * _sigmoid(zh))).astype(o_ref.dtype)


def gdn_core(proj, a, b, conv_w, a_log, dt_bias, o_gain, *, batch, seq):
    t = proj.shape[0]
    rows = GDN_ROWS
    nr = seq // rows
    cpr = rows // CHUNK
    h = GDN_HEADS
    a_t = a.reshape(batch, seq // CHUNK, CHUNK, h).transpose(0, 1, 3, 2)
    a_t = a_t.reshape(batch * nr, cpr, h, CHUNK)
    row_map = lambda bi, r: (bi * nr + r, 0)
    full = lambda bi, r: (0, 0)
    return pl.pallas_call(
        _gdn_kernel,
        grid=(batch, nr),
        in_specs=[pl.BlockSpec((rows, 3 * GDN_W), row_map),
                  pl.BlockSpec((rows, GDN_W), lambda bi, r: (bi * nr + r, 3)),
                  pl.BlockSpec((rows, h), row_map),
                  pl.BlockSpec((rows, h), row_map),
                  pl.BlockSpec((1, cpr, h, CHUNK), lambda bi, r: (bi * nr + r, 0, 0, 0)),
                  pl.BlockSpec((CONV_K, 3 * GDN_W), full),
                  pl.BlockSpec((1, h), full), pl.BlockSpec((1, h), full),
                  pl.BlockSpec((h, 1), full), pl.BlockSpec((h, 1), full),
                  pl.BlockSpec((1, GDN_HEAD_DIM), full)],
        out_specs=pl.BlockSpec((rows, GDN_W), row_map),
        out_shape=jax.ShapeDtypeStruct((t, GDN_W), BF16),
        scratch_shapes=[pltpu.VMEM((SUBLANES + rows, 3 * GDN_W), F32),
                        pltpu.VMEM((rows, 3 * GDN_W), F32),
                        pltpu.VMEM((rows, GDN_W), BF16),
                        pltpu.VMEM((rows, GDN_W), BF16),
                        pltpu.VMEM((rows, GDN_W), BF16),
                        pltpu.VMEM((rows, GDN_W), BF16),
                        pltpu.VMEM((rows, GDN_W), BF16),
                        pltpu.VMEM((rows, 2 * GDN_W), BF16),
                        pltpu.VMEM((cpr * h, CHUNK, CHUNK), F32),
                        pltpu.VMEM((cpr * h, CHUNK, CHUNK), F32),
                        pltpu.VMEM((cpr * h, CHUNK, CHUNK), BF16),
                        pltpu.VMEM((cpr * h, CHUNK, CHUNK), F32),
                        pltpu.VMEM((cpr * h, CHUNK, CHUNK), BF16),
                        pltpu.VMEM((cpr * h, CHUNK, CHUNK), BF16),
                        pltpu.VMEM((rows, GDN_W), BF16),
                        pltpu.VMEM((rows, GDN_W), F32),
                        pltpu.VMEM((h, CHUNK, GDN_HEAD_DIM), BF16),
                        pltpu.VMEM((h, GDN_HEAD_DIM, GDN_HEAD_DIM), F32),
                        pltpu.VMEM((h, GDN_HEAD_DIM, GDN_HEAD_DIM), BF16)],
        compiler_params=_cparams(("arbitrary", "arbitrary")),
        name="gdn_core",
    )(proj, proj, a, b, a_t, conv_w, a_log.reshape(1, h), dt_bias.reshape(1, h),
      a_log.reshape(h, 1), dt_bias.reshape(h, 1), o_gain.reshape(1, GDN_HEAD_DIM))


ATT_TQ = 1024
ATT_TK = 1024


def _attn_kernel(lam_ref, q_ref, k_ref, vt_ref, g_ref, o_ref, m_scr, l_scr, acc_scr, *, out_scale):
    i = pl.program_id(2)
    d = DIFF_HEAD_DIM
    tq, tk = ATT_TQ, ATT_TK
    q = q_ref[...]
    qs = [q[:, c * d:(c + 1) * d] for c in range(2)]
    m_scr[...] = jnp.full(m_scr.shape, -jnp.inf, F32)
    l_scr[...] = jnp.zeros(l_scr.shape, F32)
    acc_scr[...] = jnp.zeros(acc_scr.shape, F32)

    def tile(j, masked):
        k = k_ref[pl.ds(pl.multiple_of(j * tk, tk), tk), :]
        vt = vt_ref[0, 0, j]
        sts = [lax.dot_general(k[:, c * d:(c + 1) * d], qs[c], (((1,), (1,)), ((), ())),
                               preferred_element_type=F32) for c in range(2)]
        if masked:
            kpos = lax.broadcasted_iota(jnp.int32, (tk, tq), 0)
            qpos = lax.broadcasted_iota(jnp.int32, (tk, tq), 1)
            sts = [jnp.where(kpos <= qpos, st, -jnp.inf) for st in sts]
        ps, alphas = [], []
        for c in range(2):
            m_prev = m_scr[c]
            m_new = jnp.maximum(m_prev, jnp.max(sts[c], axis=0, keepdims=True))
            alpha = jnp.exp2(m_prev - m_new)
            p = jnp.exp2(sts[c] - m_new)
            l_scr[c] = alpha * l_scr[c] + jnp.sum(p, axis=0, keepdims=True)
            m_scr[c] = m_new
            ps.append(p.astype(BF16))
            alphas.append(alpha)
        for c in range(2):
            acc_scr[c] = alphas[c] * acc_scr[c] + jnp.dot(vt, ps[c],
                                                          preferred_element_type=F32)

    def body(j, carry):
        tile(j, False)
        return carry

    lax.fori_loop(0, i, body, 0)

    half = tk // 2
    kd = k_ref[pl.ds(pl.multiple_of(i * tk, tk), tk), :]
    vtd = vt_ref[0, 0, i]
    nt = lambda a, b: lax.dot_general(a, b, (((1,), (1,)), ((), ())), preferred_element_type=F32)
    kpos = lax.broadcasted_iota(jnp.int32, (half, tq), 0)
    qpos = lax.broadcasted_iota(jnp.int32, (half, tq), 1)
    kpos_b = lax.broadcasted_iota(jnp.int32, (half, half), 0)
    qpos_b = lax.broadcasted_iota(jnp.int32, (half, half), 1)
    zeros_l = jnp.zeros((1, half), F32)
    zeros_a = jnp.zeros((2 * d, half), F32)
    for c in range(2):
        st_a = jnp.where(kpos <= qpos, nt(kd[:half, c * d:(c + 1) * d], qs[c]), -jnp.inf)
        st_b = jnp.where(kpos_b <= qpos_b,
                         nt(kd[half:, c * d:(c + 1) * d], qs[c][half:, :]), -jnp.inf)
        m_prev = m_scr[c]
        m_a = jnp.maximum(m_prev, jnp.max(st_a, axis=0, keepdims=True))
        m_b = jnp.maximum(m_a[:, half:], jnp.max(st_b, axis=0, keepdims=True))
        m_new = jnp.concatenate([m_a[:, :half], m_b], axis=1)
        alpha = jnp.exp2(m_prev - m_new)
        p_a = jnp.exp2(st_a - m_new)
        p_b = jnp.exp2(st_b - m_b)
        l_scr[c] = (alpha * l_scr[c] + jnp.sum(p_a, axis=0, keepdims=True)
                    + jnp.concatenate([zeros_l, jnp.sum(p_b, axis=0, keepdims=True)], axis=1))
        pv_a = jnp.dot(vtd[:, :half], p_a.astype(BF16), preferred_element_type=F32)
        pv_b = jnp.dot(vtd[:, half:], p_b.astype(BF16), preferred_element_type=F32)
        acc_scr[c] = alpha * acc_scr[c] + pv_a + jnp.concatenate([zeros_a, pv_b], axis=1)
        m_scr[c] = m_new

    lam = lam_ref[0]
    o = acc_scr[0] / l_scr[0] - lam * (acc_scr[1] / l_scr[1])
    o = o * lax.rsqrt(jnp.mean(o * o, axis=0, keepdims=True) + NORM_EPS) * (g_ref[...] * out_scale)
    o_ref[...] = o.T.astype(o_ref.dtype)


def diff_attn_core(q, k, vt, lam, subln_g, *, batch, seq, out_scale):
    t, width = q.shape
    hw = 2 * DIFF_HEAD_DIM
    assert ATT_TQ == ATT_TK
    nq, nk = seq // ATT_TQ, seq // ATT_TK
    assert vt.shape == (batch, DIFF_HEADS, nk, hw, ATT_TK)
    return pl.pallas_call(
        functools.partial(_attn_kernel, out_scale=out_scale),
        grid=(batch, DIFF_HEADS, nq),
        in_specs=[pl.BlockSpec(memory_space=pltpu.SMEM),
                  pl.BlockSpec((ATT_TQ, hw), lambda b, h, i: (b * nq + i, h)),
                  pl.BlockSpec((seq, hw), lambda b, h, i: (b, h)),
                  pl.BlockSpec((1, 1, nk, hw, ATT_TK), lambda b, h, i: (b, h, 0, 0, 0)),
                  pl.BlockSpec((hw, 1), lambda b, h, i: (0, 0))],
        out_specs=pl.BlockSpec((ATT_TQ, hw), lambda b, h, i: (b * nq + i, h)),
        out_shape=jax.ShapeDtypeStruct((t, width), BF16),
        scratch_shapes=[pltpu.VMEM((2, 1, ATT_TQ), F32), pltpu.VMEM((2, 1, ATT_TQ), F32),
                        pltpu.VMEM((2, hw, ATT_TQ), F32)],
        compiler_params=_cparams(("arbitrary", "arbitrary", "arbitrary")),
        name="diff_attn",
    )(lam, q, k, vt, subln_g.reshape(hw, 1))


PEER_TT = 512
PEER_TE = 1024
_CAND_COUNTS = tuple(min(PEER_TOPK, (PEER_TOPK + 1) // (i + 1)) for i in range(PEER_TOPK))
_N_CAND = sum(_CAND_COUNTS)
_N_CAND_PAD = -(-_N_CAND // SUBLANES) * SUBLANES


def _extract_top(x, n):
    rows = x.shape[0]
    iota = lax.broadcasted_iota(jnp.int32, x.shape, 0)
    vals = []
    for _ in range(n):
        m = jnp.max(x, axis=0, keepdims=True)
        idx = jnp.min(jnp.where(x == m, iota, rows), axis=0, keepdims=True)
        x = jnp.where(iota == idx, -jnp.inf, x)
        vals.append(m)
    return vals


def _oddeven_merge_sort_pairs(n):
    pairs = []
    p = 1
    while p < n:
        k = p
        while k >= 1:
            for j in range(k % p, n - k, 2 * k):
                for i in range(min(k, n - j - k)):
                    if (i + j) // (2 * p) == (i + j + k) // (2 * p):
                        pairs.append((i + j, i + j + k))
            k //= 2
        p *= 2
    return pairs


_SORT16 = tuple(_oddeven_merge_sort_pairs(PEER_TOPK))


def _top16_sorted(x):
    n = PEER_TOPK
    vs = [x[SUBLANES * r:SUBLANES * (r + 1), :] for r in range(n)]
    for i, j in _SORT16:
        vs[i], vs[j] = jnp.maximum(vs[i], vs[j]), jnp.minimum(vs[i], vs[j])
    shift = SUBLANES // 2
    while shift >= 1:
        vs = [jnp.maximum(vs[i], pltpu.roll(vs[n - 1 - i], shift, axis=0)) for i in range(n)]
        k = n // 2
        while k >= 1:
            for i in range(n):
                if i & k == 0:
                    vs[i], vs[i + k] = jnp.maximum(vs[i], vs[i + k]), jnp.minimum(vs[i], vs[i + k])
            k //= 2
        shift //= 2
    return vs


def _peer_topk_kernel(q_ref, keys_ref, s1_ref, s2_ref, tau_ref):
    tt = q_ref.shape[0]
    kparts = [_split3(keys_ref[j])[:2] for j in range(2)]
    for p in range(PEER_HEADS):
        sc = []
        for j in range(2):
            c0 = (2 * p + j) * PEER_HALF
            qh, ql = _split3(q_ref[:, c0:c0 + PEER_HALF])[:2]
            kh, kl = kparts[j]
            nt = lambda a, b: lax.dot_general(a, b, (((1,), (1,)), ((), ())),
                                              preferred_element_type=F32)
            sc.append(nt(kh, qh) + (nt(kh, ql) + nt(kl, qh)))
        v1 = [v[0:1, :] for v in _top16_sorted(sc[0])]
        v2 = jnp.concatenate([v[0:1, :] for v in _top16_sorted(sc[1])], axis=0)
        cand = [v1[i] + v2[0:_CAND_COUNTS[i]] for i in range(PEER_TOPK)]
        if _N_CAND_PAD > _N_CAND:
            cand.append(jnp.full((_N_CAND_PAD - _N_CAND, tt), -jnp.inf, F32))
        best = _extract_top(jnp.concatenate(cand, axis=0), PEER_TOPK + 1)
        zsum = sum(jnp.exp(bk - best[0]) for bk in best[:PEER_TOPK])
        mz = best[0] + jnp.log(zsum)
        tau = 0.5 * (best[PEER_TOPK - 1] + best[PEER_TOPK])
        s1_ref[0, p] = (sc[0] - mz) * LOG2E
        s2_ref[0, p] = sc[1] * LOG2E
        tau_ref[0, p:p + 1, :] = (tau - mz) * LOG2E


def peer_topk(q, sub_keys):
    t = q.shape[0]
    tt = PEER_TT
    nt = t // tt
    big = jax.ShapeDtypeStruct((nt, PEER_HEADS, N_KEYS, tt), F32)
    return pl.pallas_call(
        _peer_topk_kernel,
        grid=(nt,),
        in_specs=[pl.BlockSpec((tt, q.shape[1]), lambda i: (i, 0)),
                  pl.BlockSpec((2, N_KEYS, PEER_HALF), lambda i: (0, 0, 0))],
        out_specs=[pl.BlockSpec((1, PEER_HEADS, N_KEYS, tt), lambda i: (i, 0, 0, 0)),
                   pl.BlockSpec((1, PEER_HEADS, N_KEYS, tt), lambda i: (i, 0, 0, 0)),
                   pl.BlockSpec((1, PEER_HEADS, tt), lambda i: (i, 0, 0))],
        out_shape=[big, big, jax.ShapeDtypeStruct((nt, PEER_HEADS, tt), F32)],
        compiler_params=_cparams(("arbitrary",)),
        name="peer_topk",
    )(q, sub_keys)


def _peer_expert_kernel(h_ref, s1_ref, s2_ref, tau_ref, u_ref, vt_ref, x_ref, gate_ref, fin_ref,
                        o_ref, acc, act_scr, w_scr, *, final_norm):
    e = pl.program_id(1)
    ne = pl.num_programs(1)
    tt = h_ref.shape[0]
    a_per = PEER_TE // N_KEYS
    gb = 32

    @pl.when(e == 0)
    def _():
        acc[...] = jnp.zeros_like(acc)

    act_scr[...] = lax.dot_general(u_ref[0], h_ref[...], (((1,), (1,)), ((), ())),
                                   preferred_element_type=F32)
    for ai in range(a_per):
        a = e * a_per + ai
        for tc in range(tt // LANES):
            cols = slice(tc * LANES, (tc + 1) * LANES)
            for bi in range(N_KEYS // gb):
                bs = slice(bi * gb, (bi + 1) * gb)
                rs = slice(ai * N_KEYS + bi * gb, ai * N_KEYS + (bi + 1) * gb)
                gsum = jnp.zeros((gb, LANES), F32)
                for p in range(PEER_HEADS):
                    s1_row = s1_ref[0, p, pl.ds(a, 1), :][:, cols]
                    tsum = s1_row + s2_ref[0, p, bs, cols]
                    gsum = gsum + jnp.where(tsum > tau_ref[0, p:p + 1, cols], jnp.exp2(tsum), 0.0)
                act = act_scr[rs, cols]
                gel = 0.5 * act * (1.0 + lax.erf(act * (2.0 ** -0.5)))
                w_scr[rs, cols] = (gsum * gel).astype(BF16)
    acc[...] += jnp.dot(vt_ref[0], w_scr[...], preferred_element_type=F32)

    @pl.when(e == ne - 1)
    def _():
        o = x_ref[...] + gate_ref[0] * acc[...].T
        if final_norm:
            o = o * lax.rsqrt(jnp.mean(o * o, axis=-1, keepdims=True) + NORM_EPS) * fin_ref[...]
        o_ref[...] = o


def peer_experts(h, s1, s2, tau, u_bf, vt_bf, x, gate, final_g, *, layer, seq, final_norm):
    t, d = h.shape
    n_exp = u_bf.shape[1]
    tt, te = PEER_TT, PEER_TE
    bidx = lambda i: (i * tt) // seq
    once = pl.Buffered(1)
    return pl.pallas_call(
        functools.partial(_peer_expert_kernel, final_norm=final_norm),
        grid=(t // tt, n_exp // te),
        in_specs=[pl.BlockSpec((tt, d), lambda i, e: (i, 0), pipeline_mode=once),
                  pl.BlockSpec((1, PEER_HEADS, N_KEYS, tt), lambda i, e: (i, 0, 0, 0),
                               pipeline_mode=once),
                  pl.BlockSpec((1, PEER_HEADS, N_KEYS, tt), lambda i, e: (i, 0, 0, 0),
                               pipeline_mode=once),
                  pl.BlockSpec((1, PEER_HEADS, tt), lambda i, e: (i, 0, 0)),
                  pl.BlockSpec((1, te, d), lambda i, e: (layer, e, 0)),
                  pl.BlockSpec((1, d, te), lambda i, e: (layer, 0, e)),
                  pl.BlockSpec((tt, d), lambda i, e: (i, 0), pipeline_mode=once),
                  pl.BlockSpec((1, 1, d), lambda i, e: (bidx(i), 0, 0)),
                  pl.BlockSpec((1, d), lambda i, e: (0, 0))],
        out_specs=pl.BlockSpec((tt, d), lambda i, e: (i, 0)),
        out_shape=jax.ShapeDtypeStruct((t, d), F32),
        scratch_shapes=[pltpu.VMEM((d, tt), F32),
                        pltpu.VMEM((te, tt), F32), pltpu.VMEM((te, tt), BF16)],
        compiler_params=_cparams(("arbitrary", "arbitrary")),
        name="peer_experts",
    )(h, s1, s2, tau, u_bf, vt_bf, x, gate, final_g)


def _rope_tables(positions):
    d = DIFF_HEAD_DIM
    inv_freq = ROPE_THETA ** (-jnp.arange(0, d, 2, dtype=F32) / d)
    ang = positions.astype(F32)[..., None] * inv_freq
    cos, sin = jnp.cos(ang), jnp.sin(ang)
    cosf = jnp.concatenate([cos, cos], axis=-1).reshape(-1, d)
    sins = jnp.concatenate([-sin, sin], axis=-1).reshape(-1, d)
    return cosf, sins


def kernel(x, c, positions, ada_w, ada_b, norm_mix_g, norm_ffn_g, gdn_w_in, gdn_conv_w, gdn_a_log,
           gdn_dt_bias, gdn_o_gain, gdn_w_out, kv_norm_g, kv_ada_w, kv_ada_b, kv_w, diff_w_q,
           diff_lambda, diff_subln_g, diff_w_out, peer_w_q, peer_sub_keys, peer_u, peer_v, final_g):
    batch, seq, d = x.shape
    depth = ada_w.shape[0]
    assert depth > N_A_LAYERS
    t = batch * seq
    xs = x.reshape(t, d)
    cosf, sins = _rope_tables(positions)

    c_pad = jnp.pad(c, ((0, SUBLANES - batch), (0, 0)))
    mods = adaln_all(c_pad, ada_w, ada_b)[:, :batch]
    kv_mods = adaln_all(c_pad, kv_ada_w[None], kv_ada_b[None])[0, :batch]
    mod = lambda m: m[:, None, :]

    u_all = peer_u.astype(BF16)
    vt_all = jnp.swapaxes(peer_v, 1, 2).astype(BF16)

    k_sh = vt_sh = None
    for l in range(depth):
        sh1, sc1, gt1, sh2, sc2, gt2 = [mod(m) for m in jnp.split(mods[l], 6, axis=-1)]
        norm1 = (norm_mix_g[l].reshape(1, d), sh1, sc1)
        if l < N_A_LAYERS:
            w_in = gdn_w_in[l]
            w4 = 4 * GDN_W
            proj = fused_matmul(xs, w_in[:, :w4].astype(BF16), seq=seq, norm=norm1, emit_h=True,
                                out_dtype=BF16)
            proj, h1 = proj
            w_ab = jnp.pad(w_in[:, w4:], ((0, 0), (0, LANES - 2 * GDN_HEADS))).astype(BF16)
            ab = fused_matmul(h1, w_ab, seq=seq)
            o = gdn_core(proj, ab[:, :GDN_HEADS],
                         ab[:, GDN_HEADS:2 * GDN_HEADS], gdn_conv_w[l], gdn_a_log[l],
                         gdn_dt_bias[l], gdn_o_gain[l], batch=batch, seq=seq)
            xs = fused_matmul(o, gdn_w_out[l].astype(BF16), seq=seq, res=(xs, gt1))
        else:
            j = l - N_A_LAYERS
            lambda_init = 0.8 - 0.6 * math.exp(-0.3 * l)
            lp = diff_lambda[j].astype(F32)
            lam = jnp.exp(jnp.sum(lp[0] * lp[1])) - jnp.exp(jnp.sum(lp[2] * lp[3])) + lambda_init
            qw = DIFF_HEADS * 2 * DIFF_HEAD_DIM
            q = fused_matmul(xs, diff_w_q[j].astype(BF16), seq=seq, norm=norm1,
                             rope=(cosf, sins, qw, DIFF_HEAD_DIM ** -0.5 * LOG2E), out_dtype=BF16)
            o = diff_attn_core(q, k_sh, vt_sh, lam.reshape(1), diff_subln_g[j], batch=batch,
                               seq=seq, out_scale=1.0 - lambda_init)
            xs = fused_matmul(o, diff_w_out[j].astype(BF16), seq=seq, res=(xs, gt1))

        norm2 = (norm_ffn_g[l].reshape(1, d), sh2, sc2)
        pq, h2 = fused_matmul(xs, peer_w_q[l].astype(BF16), seq=seq, norm=norm2, emit_h=True)
        s1, s2, tau = peer_topk(pq, peer_sub_keys[l])
        xs = peer_experts(h2, s1, s2, tau, u_all, vt_all, xs, gt2, final_g.reshape(1, d),
                          layer=l, seq=seq, final_norm=(l == depth - 1))

        if l == N_A_LAYERS - 1:
            kvsh, kvsc = [mod(m) for m in jnp.split(kv_mods, 2, axis=-1)]
            qw = DIFF_HEADS * 2 * DIFF_HEAD_DIM
            k_sh, vt_sh = fused_matmul(xs, kv_w.astype(BF16), seq=seq,
                                       norm=(kv_norm_g.reshape(1, d), kvsh, kvsc),
                                       rope=(cosf, sins, qw, 1.0), out_dtype=BF16,
                                       tail_hw=2 * DIFF_HEAD_DIM, tm=ATT_TK)

    return xs.reshape(batch, seq, d)
```
